```python
import math
import jax, jax.numpy as jnp
from jax import lax
import numpy as np

D_MODEL = 4096
BATCH = 4
SEQ = 2048
DEPTH = 2

D_SSM = D_MODEL // 2
D_RET = D_MODEL // 2
D_MIX = D_SSM + D_RET
SSM_GROUP = 16
N_SSM_GROUPS = D_SSM // SSM_GROUP
SSM_STATE = 64
RET_HEADS = 8
RET_HEAD_DIM = D_RET // RET_HEADS
RET_CHUNK = 128
ROPE_BASE = 10000.0
EPS = 1e-6
DT_MIN = 0.001
DT_MAX = 0.1
D_IN_PROJ = 2 * D_SSM + 4 * D_RET

kernel_name = "hymba_s5_retnet_hybrid"


def rmsnorm(x, w):
    xf = x.astype(jnp.float32)
    y = xf * lax.rsqrt(jnp.mean(xf * xf, axis=-1, keepdims=True) + EPS)
    return (y * w.astype(jnp.float32)).astype(x.dtype)


def s5_mixer(u, lam_re, lam_im, b_re, b_im, c_re, c_im, d_skip, log_dt, w_glu, b_glu):
    f32 = jnp.float32
    bsz, seq, _ = u.shape
    ug = u.astype(f32).reshape(bsz, seq, N_SSM_GROUPS, SSM_GROUP)
    lr = lam_re.astype(f32)
    li = lam_im.astype(f32)
    dt = jnp.exp(log_dt.astype(f32))[:, None]
    mag = jnp.exp(lr * dt)
    ab_re = mag * jnp.cos(li * dt)
    ab_im = mag * jnp.sin(li * dt)
    num_re = ab_re - 1.0
    num_im = ab_im
    den = lr * lr + li * li
    co_re = (num_re * lr + num_im * li) / den
    co_im = (num_im * lr - num_re * li) / den
    br = b_re.astype(f32)
    bi = b_im.astype(f32)
    bb_re = co_re[..., None] * br - co_im[..., None] * bi
    bb_im = co_re[..., None] * bi + co_im[..., None] * br
    bu_re = jnp.einsum('blgh,gnh->blgn', ug, bb_re)
    bu_im = jnp.einsum('blgh,gnh->blgn', ug, bb_im)
    a_re = jnp.broadcast_to(ab_re, (1, seq) + ab_re.shape)
    a_im = jnp.broadcast_to(ab_im, (1, seq) + ab_im.shape)

    def combine(e1, e2):
        a1r, a1i, b1r, b1i = e1
        a2r, a2i, b2r, b2i = e2
        return (a2r * a1r - a2i * a1i,
                a2r * a1i + a2i * a1r,
                a2r * b1r - a2i * b1i + b2r,
                a2r * b1i + a2i * b1r + b2i)

    _, _, s_re, s_im = lax.associative_scan(combine, (a_re, a_im, bu_re, bu_im), axis=1)
    y = (jnp.einsum('blgn,ghn->blgh', s_re, c_re.astype(f32))
         - jnp.einsum('blgn,ghn->blgh', s_im, c_im.astype(f32))
         + d_skip.astype(f32).reshape(N_SSM_GROUPS, SSM_GROUP) * ug)
    y = jax.nn.gelu(y.reshape(bsz, seq, D_SSM))
    y = y * jax.nn.sigmoid(y @ w_glu.astype(f32) + b_glu.astype(f32))
    return y


def rope(t, cos, sin):
    half = t.shape[-1] // 2
    t1, t2 = t[..., :half], t[..., half:]
    return jnp.concatenate([t1 * cos - t2 * sin, t2 * cos + t1 * sin], axis=-1)


def retention_mixer(q, k, v):
    f32 = jnp.float32
    bsz, seq, _ = q.shape
    nc = seq // RET_CHUNK

    def heads(t):
        return t.astype(f32).reshape(bsz, seq, RET_HEADS, RET_HEAD_DIM).transpose(0, 2, 1, 3)

    q, k, v = heads(q), heads(k), heads(v)
    pos = jnp.arange(seq, dtype=f32)
    inv_freq = ROPE_BASE ** (-jnp.arange(0, RET_HEAD_DIM, 2, dtype=f32) / RET_HEAD_DIM)
    ang = pos[:, None] * inv_freq[None, :]
    cos, sin = jnp.cos(ang), jnp.sin(ang)
    q = rope(q, cos, sin)
    k = rope(k, cos, sin) * (RET_HEAD_DIM ** -0.5)
    log_gamma = jnp.log1p(-jnp.power(2.0, -5.0 - jnp.arange(RET_HEADS, dtype=f32)))

    qc = q.reshape(bsz, RET_HEADS, nc, RET_CHUNK, RET_HEAD_DIM)
    kc = k.reshape(bsz, RET_HEADS, nc, RET_CHUNK, RET_HEAD_DIM)
    vc = v.reshape(bsz, RET_HEADS, nc, RET_CHUNK, RET_HEAD_DIM)
    idx = jnp.arange(RET_CHUNK, dtype=f32)
    rel = idx[:, None] - idx[None, :]
    causal = rel >= 0
    decay_in = jnp.where(causal[None],
                         jnp.exp(jnp.where(causal, rel, 0.0)[None] * log_gamma[:, None, None]),
                         0.0)
    scores = jnp.einsum('bhnid,bhnjd->bhnij', qc, kc) * decay_in[None, :, None]
    out_intra = jnp.einsum('bhnij,bhnje->bhnie', scores, vc)
    zeta = jnp.exp((RET_CHUNK - 1.0 - idx)[None, :] * log_gamma[:, None])
    u_chunk = jnp.einsum('bhnjd,bhnje,hj->bhnde', kc, vc, zeta)
    gamma_c = jnp.exp(RET_CHUNK * log_gamma)

    def step(r, u_i):
        return r * gamma_c[None, :, None, None] + u_i, r

    r0 = jnp.zeros((bsz, RET_HEADS, RET_HEAD_DIM, RET_HEAD_DIM), f32)
    _, r_prev = lax.scan(step, r0, jnp.moveaxis(u_chunk, 2, 0))
    xi = jnp.exp((idx + 1.0)[None, :] * log_gamma[:, None])
    out_cross = jnp.einsum('bhnid,nbhde->bhnie', qc, r_prev) * xi[None, :, None, :, None]
    out = (out_intra + out_cross).reshape(bsz, RET_HEADS, seq, RET_HEAD_DIM)
    out = out * lax.rsqrt(jnp.mean(out * out, axis=-1, keepdims=True) + EPS)
    return out.transpose(0, 2, 1, 3).reshape(bsz, seq, D_RET)


def hybrid_layer(x, norm_w, w_in, lam_re, lam_im, b_re, b_im, c_re, c_im, d_skip,
                 log_dt, w_glu, b_glu, ssm_norm_w, ret_norm_w, w_out):
    f32 = jnp.float32
    h = rmsnorm(x, norm_w)
    proj = h @ w_in
    o1 = D_SSM
    o2 = o1 + D_SSM
    o3 = o2 + D_RET
    o4 = o3 + D_RET
    o5 = o4 + D_RET
    u, g_ssm, q, k, v, g_ret = (proj[..., :o1], proj[..., o1:o2], proj[..., o2:o3],
                                proj[..., o3:o4], proj[..., o4:o5], proj[..., o5:])
    y_ssm = s5_mixer(u, lam_re, lam_im, b_re, b_im, c_re, c_im, d_skip, log_dt, w_glu, b_glu)
    y_ssm = rmsnorm(y_ssm, ssm_norm_w) * jax.nn.silu(g_ssm.astype(f32))
    y_ret = retention_mixer(q, k, v) * ret_norm_w.astype(f32) * jax.nn.silu(g_ret.astype(f32))
    y = jnp.concatenate([y_ssm, y_ret], axis=-1).astype(x.dtype)
    return x + y @ w_out


def setup_inputs(seed: int = 0) -> dict:
    key = jax.random.key(seed)
    ks = jax.random.split(key, 20)
    f32 = jnp.float32
    G, N, H = N_SSM_GROUPS, SSM_STATE, SSM_GROUP
    x = jax.random.normal(ks[0], (BATCH, SEQ, D_MODEL), f32)
    norm_w = 1.0 + 0.02 * jax.random.normal(ks[1], (DEPTH, D_MODEL), f32)
    w_in = jax.random.normal(ks[2], (DEPTH, D_MODEL, D_IN_PROJ), f32) * (D_MODEL ** -0.5)
    ssm_lambda_re = -0.5 + 0.01 * jax.random.normal(ks[3], (DEPTH, G, N), f32)
    ssm_lambda_im = (jnp.pi * jnp.arange(N, dtype=f32))[None, None, :] \
        + 0.01 * jax.random.normal(ks[4], (DEPTH, G, N), f32)
    ssm_b_re = jax.random.normal(ks[5], (DEPTH, G, N, H), f32) * ((2.0 * H) ** -0.5)
    ssm_b_im = jax.random.normal(ks[6], (DEPTH, G, N, H), f32) * ((2.0 * H) ** -0.5)
    ssm_c_re = jax.random.normal(ks[7], (DEPTH, G, H, N), f32) * (0.5 ** 0.5)
    ssm_c_im = jax.random.normal(ks[8], (DEPTH, G, H, N), f32) * (0.5 ** 0.5)
    ssm_d = jax.random.normal(ks[9], (DEPTH, D_SSM), f32)
    ssm_log_dt = jax.random.uniform(ks[10], (DEPTH, G), f32,
                                    minval=math.log(DT_MIN), maxval=math.log(DT_MAX))
    ssm_w_glu = jax.random.normal(ks[11], (DEPTH, D_SSM, D_SSM), f32) * (D_SSM ** -0.5)
    ssm_b_glu = 0.01 * jax.random.normal(ks[12], (DEPTH, D_SSM), f32)
    ssm_norm_w = 1.0 + 0.02 * jax.random.normal(ks[13], (DEPTH, D_SSM), f32)
    ret_norm_w = 1.0 + 0.02 * jax.random.normal(ks[14], (DEPTH, D_RET), f32)
    w_out = jax.random.normal(ks[15], (DEPTH, D_MIX, D_MODEL), f32) * (D_MIX ** -0.5)
    final_norm_w = 1.0 + 0.02 * jax.random.normal(ks[16], (D_MODEL,), f32)
    return {"x": x, "norm_w": norm_w, "w_in": w_in,
            "ssm_lambda_re": ssm_lambda_re, "ssm_lambda_im": ssm_lambda_im,
            "ssm_b_re": ssm_b_re, "ssm_b_im": ssm_b_im,
            "ssm_c_re": ssm_c_re, "ssm_c_im": ssm_c_im,
            "ssm_d": ssm_d, "ssm_log_dt": ssm_log_dt,
            "ssm_w_glu": ssm_w_glu, "ssm_b_glu": ssm_b_glu,
            "ssm_norm_w": ssm_norm_w, "ret_norm_w": ret_norm_w,
            "w_out": w_out, "final_norm_w": final_norm_w}


def reference(x, norm_w, w_in, ssm_lambda_re, ssm_lambda_im, ssm_b_re, ssm_b_im,
              ssm_c_re, ssm_c_im, ssm_d, ssm_log_dt, ssm_w_glu, ssm_b_glu,
              ssm_norm_w, ret_norm_w, w_out, final_norm_w):
    h = x
    for layer in range(DEPTH):
        h = hybrid_layer(h, norm_w[layer], w_in[layer],
                         ssm_lambda_re[layer], ssm_lambda_im[layer],
                         ssm_b_re[layer], ssm_b_im[layer],
                         ssm_c_re[layer], ssm_c_im[layer],
                         ssm_d[layer], ssm_log_dt[layer],
                         ssm_w_glu[layer], ssm_b_glu[layer],
                         ssm_norm_w[layer], ret_norm_w[layer], w_out[layer])
    return rmsnorm(h, final_norm_w)
```

```python
import functools
import math

import jax
import jax.numpy as jnp
from jax import lax
from jax.experimental import pallas as pl
from jax.experimental.pallas import tpu as pltpu

F32 = jnp.float32
BF16 = jnp.bfloat16

EPS = 1e-6
SSM_GROUP = 16
RET_HEADS = 8
RET_CHUNK = 128
ROPE_BASE = 10000.0
S5_CHUNK = 16

VMEM_LIMIT = 56 * 1024 * 1024


def _params(sem):
    return pltpu.CompilerParams(dimension_semantics=sem, vmem_limit_bytes=VMEM_LIMIT)


def _rmsnorm_kernel(x_ref, w_ref, o_ref):
    x = x_ref[...]
    ms = jnp.mean(x * x, axis=-1, keepdims=True)
    o_ref[...] = (x * lax.rsqrt(ms + EPS) * w_ref[...]).astype(o_ref.dtype)


def _rmsnorm(x, w, out_dtype, tm=512):
    t, d = x.shape
    tm = min(tm, t)
    return pl.pallas_call(
        _rmsnorm_kernel,
        grid=(t // tm,),
        in_specs=[pl.BlockSpec((tm, d), lambda i: (i, 0)),
                  pl.BlockSpec((1, d), lambda i: (0, 0))],
        out_specs=pl.BlockSpec((tm, d), lambda i: (i, 0)),
        out_shape=jax.ShapeDtypeStruct((t, d), out_dtype),
        compiler_params=_params(("parallel",)),
    )(x, w.reshape(1, d))


def _matmul_kernel(a_ref, b_ref, o_ref):
    o_ref[...] = jnp.dot(a_ref[...], b_ref[...], preferred_element_type=F32).astype(o_ref.dtype)


def _matmul(a, b, out_dtype, tm=1024, tn=1024):
    m, k = a.shape
    _, n = b.shape
    tm, tn = min(tm, m), min(tn, n)
    return pl.pallas_call(
        _matmul_kernel,
        grid=(m // tm, n // tn),
        in_specs=[pl.BlockSpec((tm, k), lambda i, j: (i, 0)),
                  pl.BlockSpec((k, tn), lambda i, j: (0, j))],
        out_specs=pl.BlockSpec((tm, tn), lambda i, j: (i, j)),
        out_shape=jax.ShapeDtypeStruct((m, n), out_dtype),
        compiler_params=_params(("parallel", "parallel")),
    )(a, b)


def _outproj_kernel(ys_ref, yr_ref, ws_ref, wr_ref, x_ref, o_ref):
    acc = jnp.dot(ys_ref[...], ws_ref[...], preferred_element_type=F32)
    acc = acc + jnp.dot(yr_ref[...], wr_ref[...], preferred_element_type=F32)
    o_ref[...] = x_ref[...] + acc


def _outproj(y_ssm, y_ret, w_out, x, tm=1024, tn=512):
    t, ds = y_ssm.shape
    _, dr = y_ret.shape
    assert ds == dr
    _, n = w_out.shape
    tm, tn = min(tm, t), min(tn, n)
    return pl.pallas_call(
        _outproj_kernel,
        grid=(t // tm, n // tn),
        in_specs=[pl.BlockSpec((tm, ds), lambda i, j: (i, 0)),
                  pl.BlockSpec((tm, dr), lambda i, j: (i, 0)),
                  pl.BlockSpec((ds, tn), lambda i, j: (0, j)),
                  pl.BlockSpec((dr, tn), lambda i, j: (1, j)),
                  pl.BlockSpec((tm, tn), lambda i, j: (i, j))],
        out_specs=pl.BlockSpec((tm, tn), lambda i, j: (i, j)),
        out_shape=jax.ShapeDtypeStruct((t, n), F32),
        compiler_params=_params(("parallel", "parallel")),
    )(y_ssm, y_ret, w_out, w_out, x)


def _s5_matrices(lam_re, lam_im, b_re, b_im, c_re, c_im, d_skip, log_dt, n_chunks):
    g, n = lam_re.shape
    h = SSM_GROUP
    c = S5_CHUNK
    hi = lax.Precision.HIGHEST
    dt = jnp.exp(log_dt)[:, None]
    tau = jnp.arange(c + 1, dtype=F32)[:, None, None]
    mag = jnp.exp(tau * (lam_re * dt)[None])
    ang = tau * (lam_im * dt)[None]
    p_re, p_im = mag * jnp.cos(ang), mag * jnp.sin(ang)
    ab_re, ab_im = p_re[1], p_im[1]
    num_re, num_im = ab_re - 1.0, ab_im
    den = lam_re * lam_re + lam_im * lam_im
    co_re = (num_re * lam_re + num_im * lam_im) / den
    co_im = (num_im * lam_re - num_re * lam_im) / den
    bb_re = co_re[..., None] * b_re - co_im[..., None] * b_im
    bb_im = co_re[..., None] * b_im + co_im[..., None] * b_re
    cp_re = c_re[None] * p_re[:, :, None, :] - c_im[None] * p_im[:, :, None, :]
    cp_im = c_re[None] * p_im[:, :, None, :] + c_im[None] * p_re[:, :, None, :]
    kern = (jnp.einsum('tghn,gnk->tgkh', cp_re[:c], bb_re, precision=hi)
            - jnp.einsum('tghn,gnk->tgkh', cp_im[:c], bb_im, precision=hi))
    lag = jnp.arange(c)[None, :] - jnp.arange(c)[:, None]
    toep = kern[jnp.clip(lag, 0, c - 1)]
    toep = jnp.where((lag >= 0)[:, :, None, None, None], toep, 0.0)
    m_mat = toep.transpose(2, 0, 3, 1, 4).reshape(g, c * h, c * h)
    q_re, q_im = p_re[:c][::-1], p_im[:c][::-1]
    e_re = q_re[:, :, :, None] * bb_re[None] - q_im[:, :, :, None] * bb_im[None]
    e_im = q_re[:, :, :, None] * bb_im[None] + q_im[:, :, :, None] * bb_re[None]
    e_mat = jnp.concatenate([e_re, e_im], axis=2).transpose(1, 0, 3, 2).reshape(g, c * h, 2 * n)
    f_re = cp_re[1:].transpose(1, 3, 0, 2).reshape(g, n, c * h)
    f_im = -cp_im[1:].transpose(1, 3, 0, 2).reshape(g, n, c * h)
    f_mat = jnp.concatenate([f_re, f_im], axis=1)
    n_steps = max(1, int(math.log2(n_chunks)))
    assert n_steps <= 8
    ar, ai = p_re[c], p_im[c]
    rows1, rows2 = [], []
    for _ in range(n_steps):
        rows1.append(jnp.concatenate([ar, ar], axis=-1))
        rows2.append(jnp.concatenate([-ai, ai], axis=-1))
        ar, ai = ar * ar - ai * ai, 2.0 * ar * ai
    pad = [jnp.zeros((g, 2 * n), F32)] * (8 - n_steps)
    a_pow = jnp.stack(rows1 + pad + rows2 + pad, axis=1)
    d_row = jnp.tile(d_skip.reshape(g, 1, h), (1, 1, c))
    return m_mat.astype(BF16), e_mat.astype(BF16), f_mat.astype(BF16), a_pow, d_row


def _s5_kernel(u_ref, m_ref, e_ref, f_ref, ap_ref, d_ref, y_ref, *, n_chunks):
    u = u_ref[0]
    y = jnp.dot(u, m_ref[0], preferred_element_type=F32)
    s = jnp.dot(u, e_ref[0], preferred_element_type=F32)
    rows, lanes = s.shape
    half = lanes // 2
    chunk = lax.broadcasted_iota(jnp.int32, (rows, lanes), 0) % n_chunks
    ap = ap_ref[0]
    z = s
    step = 1
    m = 0
    while step < n_chunks:
        zs = jnp.where(chunk >= step, pltpu.roll(z, step, axis=0), 0.0)
        z = z + zs * ap[m:m + 1] + pltpu.roll(zs, half, axis=1) * ap[8 + m:9 + m]
        step *= 2
        m += 1
    s_prev = jnp.where(chunk >= 1, pltpu.roll(z, 1, axis=0), 0.0)
    y = y + jnp.dot(s_prev.astype(BF16), f_ref[0], preferred_element_type=F32)
    y = y + u.astype(F32) * d_ref[0]
    y_ref[0] = y.astype(y_ref.dtype)


def _s5(u_chunks, mats, n_chunks):
    g, r, p = u_chunks.shape
    m_mat, e_mat, f_mat, a_pow, d_row = mats
    n2 = e_mat.shape[-1]
    spec3 = lambda a, b: pl.BlockSpec((1, a, b), lambda i: (i, 0, 0))
    return pl.pallas_call(
        functools.partial(_s5_kernel, n_chunks=n_chunks),
        grid=(g,),
        in_specs=[spec3(r, p), spec3(p, p), spec3(p, n2), spec3(n2, p), spec3(16, n2), spec3(1, p)],
        out_specs=spec3(r, p),
        out_shape=jax.ShapeDtypeStruct((g, r, p), BF16),
        compiler_params=_params(("parallel",)),
    )(u_chunks, m_mat, e_mat, f_mat, a_pow, d_row)


def _glu_kernel(y_ref, w_ref, b_ref, nw_ref, g_ref, o_ref):
    y = y_ref[...].astype(F32)
    y = y * (0.5 * (1.0 + jnp.tanh(math.sqrt(2.0 / math.pi) * (y + 0.044715 * (y * y * y)))))
    z = jnp.dot(y.astype(BF16), w_ref[...], preferred_element_type=F32) + b_ref[...]
    y = y * jax.nn.sigmoid(z)
    ms = jnp.mean(y * y, axis=-1, keepdims=True)
    y = y * lax.rsqrt(ms + EPS) * nw_ref[...]
    g = g_ref[...].astype(F32)
    o_ref[...] = (y * (g * jax.nn.sigmoid(g))).astype(o_ref.dtype)


def _glu(y_pre, w_glu, b_glu, norm_w, proj, gate_col_block, tm=256):
    t, d = y_pre.shape
    tm = min(tm, t)
    return pl.pallas_call(
        _glu_kernel,
        grid=(t // tm,),
        in_specs=[pl.BlockSpec((tm, d), lambda i: (i, 0)),
                  pl.BlockSpec((d, d), lambda i: (0, 0)),
                  pl.BlockSpec((1, d), lambda i: (0, 0)),
                  pl.BlockSpec((1, d), lambda i: (0, 0)),
                  pl.BlockSpec((tm, d), lambda i: (i, gate_col_block))],
        out_specs=pl.BlockSpec((tm, d), lambda i: (i, 0)),
        out_shape=jax.ShapeDtypeStruct((t, d), BF16),
        compiler_params=_params(("parallel",)),
    )(y_pre, w_glu, b_glu.reshape(1, d), norm_w.reshape(1, d), proj)


def _retention_tables(seq, head_dim):
    pos = jnp.arange(seq, dtype=F32)
    inv_freq = ROPE_BASE ** (-jnp.arange(0, head_dim, 2, dtype=F32) / head_dim)
    ang = pos[:, None] * inv_freq[None, :]
    cos, sin = jnp.cos(ang), jnp.sin(ang)
    log_gamma = jnp.log1p(-jnp.power(2.0, -5.0 - jnp.arange(RET_HEADS, dtype=F32)))
    idx = jnp.arange(RET_CHUNK, dtype=F32)
    rel = idx[:, None] - idx[None, :]
    causal = rel >= 0
    decay_in = jnp.where(causal[None],
                         jnp.exp(jnp.where(causal, rel, 0.0)[None] * log_gamma[:, None, None]),
                         0.0)
    zeta = jnp.exp((RET_CHUNK - 1.0 - idx)[None, :] * log_gamma[:, None])
    xi = jnp.exp((idx + 1.0)[None, :] * log_gamma[:, None])
    gamma_c = jnp.exp(RET_CHUNK * log_gamma)
    bcast = lambda a: jnp.broadcast_to(a[:, :, None], a.shape + (head_dim,))
    gamma_row = jnp.broadcast_to(gamma_c[:, None, None], (RET_HEADS, 1, head_dim))
    return cos, sin, decay_in, bcast(zeta), bcast(xi), gamma_row


def _retention_kernel(q_ref, k_ref, v_ref, g_ref, cos_ref, sin_ref, dec_ref, zeta_ref, xi_ref,
                      gam_ref, nw_ref, o_ref, r_ref, *, n_chunks, head_dim):
    half = head_dim // 2
    scale = head_dim ** -0.5
    r_ref[...] = jnp.zeros_like(r_ref)

    def rope(t, cos, sin):
        t1, t2 = t[:, :half], t[:, half:]
        return jnp.concatenate([t1 * cos - t2 * sin, t2 * cos + t1 * sin], axis=-1)

    def body(c, carry):
        rows = pl.ds(pl.multiple_of(c * RET_CHUNK, RET_CHUNK), RET_CHUNK)
        cos, sin = cos_ref[rows, :], sin_ref[rows, :]
        q = rope(q_ref[rows, :].astype(F32), cos, sin)
        k = rope(k_ref[rows, :].astype(F32), cos, sin) * scale
        v = v_ref[rows, :]
        qb = q.astype(BF16)
        scores = lax.dot_general(qb, k.astype(BF16), (((1,), (1,)), ((), ())),
                                 preferred_element_type=F32) * dec_ref[0]
        out = jnp.dot(scores.astype(BF16), v, preferred_element_type=F32)
        out = out + jnp.dot(qb, r_ref[...].astype(BF16), preferred_element_type=F32) * xi_ref[0]
        kz = (k * zeta_ref[0]).T.astype(BF16)
        r_ref[...] = r_ref[...] * gam_ref[0] + jnp.dot(kz, v, preferred_element_type=F32)
        ms = jnp.mean(out * out, axis=-1, keepdims=True)
        g = g_ref[rows, :].astype(F32)
        out = out * lax.rsqrt(ms + EPS) * nw_ref[...] * (g * jax.nn.sigmoid(g))
        o_ref[rows, :] = out.astype(o_ref.dtype)
        return carry

    lax.fori_loop(0, n_chunks, body, 0)


def _retention(proj, ret_norm_w, tables, batch, seq, col0):
    t = proj.shape[0]
    d_ret = ret_norm_w.shape[0]
    hd = d_ret // RET_HEADS
    cos, sin, decay_in, zeta, xi, gamma_row = tables
    n_chunks = seq // RET_CHUNK

    def col(group):
        return pl.BlockSpec((seq, hd), lambda b, h: (b, col0 + group * RET_HEADS + h))

    per_head = lambda a, b_: pl.BlockSpec((1, a, b_), lambda b, h: (h, 0, 0))
    const2 = lambda a, b_: pl.BlockSpec((a, b_), lambda b, h: (0, 0))
    return pl.pallas_call(
        functools.partial(_retention_kernel, n_chunks=n_chunks, head_dim=hd),
        grid=(batch, RET_HEADS),
        in_specs=[col(0), col(1), col(2), col(3),
                  const2(seq, hd // 2), const2(seq, hd // 2),
                  per_head(RET_CHUNK, RET_CHUNK), per_head(RET_CHUNK, hd), per_head(RET_CHUNK, hd),
                  per_head(1, hd),
                  pl.BlockSpec((1, hd), lambda b, h: (0, h))],
        out_specs=pl.BlockSpec((seq, hd), lambda b, h: (b, h)),
        out_shape=jax.ShapeDtypeStruct((t, d_ret), BF16),
        scratch_shapes=[pltpu.VMEM((hd, hd), F32)],
        compiler_params=_params(("parallel", "parallel")),
    )(proj, proj, proj, proj, cos, sin, decay_in, zeta, xi, gamma_row, ret_norm_w.reshape(1, d_ret))


def _layer(x2, batch, seq, norm_w, w_in, lam_re, lam_im, b_re, b_im, c_re, c_im, d_skip, log_dt,
           w_glu, b_glu, ssm_norm_w, ret_norm_w, w_out, ret_tables):
    t, d_model = x2.shape
    d_ssm = d_skip.shape[0]
    d_ret = ret_norm_w.shape[0]
    groups = d_ssm // SSM_GROUP
    hd = d_ret // RET_HEADS
    n_chunks = seq // S5_CHUNK
    p = S5_CHUNK * SSM_GROUP

    h = _rmsnorm(x2, norm_w, BF16)
    proj = _matmul(h, w_in.astype(BF16), BF16)

    u = proj[:, :d_ssm].reshape(batch * n_chunks, S5_CHUNK, groups, SSM_GROUP)
    u = u.transpose(2, 0, 1, 3).reshape(groups, batch * n_chunks, p)
    mats = _s5_matrices(lam_re, lam_im, b_re, b_im, c_re, c_im, d_skip, log_dt, n_chunks)
    y = _s5(u, mats, n_chunks)
    y = y.reshape(groups, batch * n_chunks, S5_CHUNK, SSM_GROUP).transpose(1, 2, 0, 3).reshape(t, d_ssm)
    y_ssm = _glu(y, w_glu.astype(BF16), b_glu, ssm_norm_w, proj, 1)

    y_ret = _retention(proj, ret_norm_w, ret_tables, batch, seq, (2 * d_ssm) // hd)

    return _outproj(y_ssm, y_ret, w_out.astype(BF16), x2)


def kernel(x, norm_w, w_in, ssm_lambda_re, ssm_lambda_im, ssm_b_re, ssm_b_im, ssm_c_re, ssm_c_im,
           ssm_d, ssm_log_dt, ssm_w_glu, ssm_b_glu, ssm_norm_w, ret_norm_w, w_out, final_norm_w):
    batch, seq, d_model = x.shape
    depth = norm_w.shape[0]
    d_ret = ret_norm_w.shape[1]
    ret_tables = _retention_tables(seq, d_ret // RET_HEADS)
    h = x.reshape(batch * seq, d_model)
    for layer in range(depth):
        h = _layer(h, batch, seq, norm_w[layer], w_in[layer],
                   ssm_lambda_re[layer], ssm_lambda_im[layer], ssm_b_re[layer], ssm_b_im[layer],
                   ssm_c_re[layer], ssm_c_im[layer], ssm_d[layer], ssm_log_dt[layer],
                   ssm_w_glu[layer], ssm_b_glu[layer], ssm_norm_w[layer], ret_norm_w[layer],
                   w_out[layer], ret_tables)
    out = _rmsnorm(h, final_norm_w, F32)
    return out.reshape(batch, seq, d_model)
```

```python
import functools
import math

import jax
import jax.numpy as jnp
from jax import lax
from jax.experimental import pallas as pl
from jax.experimental.pallas import tpu as pltpu

F32 = jnp.float32
BF16 = jnp.bfloat16

EPS = 1e-6
SSM_GROUP = 16
RET_HEADS = 8
RET_CHUNK = 128
ROPE_BASE = 10000.0
S5_CHUNK = 16

VMEM_LIMIT = 56 * 1024 * 1024


def _params(sem):
    return pltpu.CompilerParams(dimension_semantics=sem, vmem_limit_bytes=VMEM_LIMIT)


def _single(block_shape, index_map):
    return pl.BlockSpec(block_shape, index_map, pipeline_mode=pl.Buffered(1))


def _rmsnorm_kernel(x_ref, w_ref, o_ref):
    x = x_ref[...]
    ms = jnp.mean(x * x, axis=-1, keepdims=True)
    o_ref[...] = (x * lax.rsqrt(ms + EPS) * w_ref[...]).astype(o_ref.dtype)


def _rmsnorm(x, w, out_dtype, tm=512):
    t, d = x.shape
    tm = min(tm, t)
    return pl.pallas_call(
        _rmsnorm_kernel,
        grid=(t // tm,),
        in_specs=[pl.BlockSpec((tm, d), lambda i: (i, 0)),
                  pl.BlockSpec((1, d), lambda i: (0, 0))],
        out_specs=pl.BlockSpec((tm, d), lambda i: (i, 0)),
        out_shape=jax.ShapeDtypeStruct((t, d), out_dtype),
        compiler_params=_params(("parallel",)),
    )(x, w.reshape(1, d))


def _inproj_kernel(a_ref, w_ref, o_ref, wb_ref):
    @pl.when(pl.program_id(1) == 0)
    def _():
        wb_ref[...] = w_ref[...].astype(BF16)

    o_ref[...] = jnp.dot(a_ref[...], wb_ref[...], preferred_element_type=F32).astype(o_ref.dtype)


def _inproj(h, w_in, layer, col_start, n_cols, tm=1024, tn=1024):
    t, k = h.shape
    tm, tn = min(tm, t), min(tn, n_cols)
    assert col_start % tn == 0
    c0 = col_start // tn
    return pl.pallas_call(
        _inproj_kernel,
        grid=(n_cols // tn, t // tm),
        in_specs=[pl.BlockSpec((tm, k), lambda j, i: (i, 0)),
                  _single((None, k, tn), lambda j, i: (layer, 0, c0 + j))],
        out_specs=pl.BlockSpec((tm, tn), lambda j, i: (i, j)),
        out_shape=jax.ShapeDtypeStruct((t, n_cols), BF16),
        scratch_shapes=[pltpu.VMEM((k, tn), BF16)],
        compiler_params=_params(("arbitrary", "arbitrary")),
    )(h, w_in)


def _inproj_t_kernel(wt_ref, h_ref, o_ref):
    acc = lax.dot_general(wt_ref[...], h_ref[...], (((1,), (1,)), ((), ())),
                          preferred_element_type=F32)
    o_ref[...] = acc.astype(BF16).reshape(o_ref.shape)


def _inproj_t(w_t, h, n_rows):
    ch, k = w_t.shape
    groups = ch // SSM_GROUP
    h_slabs = h.reshape(n_rows, S5_CHUNK * k)
    return pl.pallas_call(
        _inproj_t_kernel,
        grid=(S5_CHUNK,),
        in_specs=[_single((ch, k), lambda j: (0, 0)),
                  pl.BlockSpec((n_rows, k), lambda j: (0, j))],
        out_specs=pl.BlockSpec((groups, SSM_GROUP, n_rows), lambda j: (0, j, 0)),
        out_shape=jax.ShapeDtypeStruct((groups, S5_CHUNK * SSM_GROUP, n_rows), BF16),
        compiler_params=_params(("parallel",)),
    )(w_t, h_slabs)


def _outproj_kernel(ys_ref, yr_ref, ws_ref, wr_ref, x_ref, o_ref, wsb_ref, wrb_ref):
    @pl.when(pl.program_id(1) == 0)
    def _():
        wsb_ref[...] = ws_ref[...].astype(BF16)
        wrb_ref[...] = wr_ref[...].astype(BF16)

    acc = jnp.dot(ys_ref[...], wsb_ref[...], preferred_element_type=F32)
    acc = acc + jnp.dot(yr_ref[...], wrb_ref[...], preferred_element_type=F32)
    o_ref[...] = x_ref[...] + acc


def _outproj(y_ssm, y_ret, w_out, layer, x, tm=1024, tn=512):
    t, ds = y_ssm.shape
    _, dr = y_ret.shape
    assert ds == dr
    n = w_out.shape[-1]
    tm, tn = min(tm, t), min(tn, n)
    return pl.pallas_call(
        _outproj_kernel,
        grid=(n // tn, t // tm),
        in_specs=[pl.BlockSpec((tm, ds), lambda j, i: (i, 0)),
                  pl.BlockSpec((tm, dr), lambda j, i: (i, 0)),
                  _single((None, ds, tn), lambda j, i: (layer, 0, j)),
                  _single((None, dr, tn), lambda j, i: (layer, 1, j)),
                  pl.BlockSpec((tm, tn), lambda j, i: (i, j))],
        out_specs=pl.BlockSpec((tm, tn), lambda j, i: (i, j)),
        out_shape=jax.ShapeDtypeStruct((t, n), F32),
        scratch_shapes=[pltpu.VMEM((ds, tn), BF16), pltpu.VMEM((dr, tn), BF16)],
        compiler_params=_params(("arbitrary", "arbitrary")),
    )(y_ssm, y_ret, w_out, w_out, x)


def _s5_matrices(lam_re, lam_im, b_re, b_im, c_re, c_im, d_skip, log_dt, n_chunks):
    g, n = lam_re.shape
    h = SSM_GROUP
    c = S5_CHUNK
    hi = lax.Precision.HIGHEST
    dt = jnp.exp(log_dt)[:, None]
    tau = jnp.arange(c + 1, dtype=F32)[:, None, None]
    mag = jnp.exp(tau * (lam_re * dt)[None])
    ang = tau * (lam_im * dt)[None]
    p_re, p_im = mag * jnp.cos(ang), mag * jnp.sin(ang)
    ab_re, ab_im = p_re[1], p_im[1]
    num_re, num_im = ab_re - 1.0, ab_im
    den = lam_re * lam_re + lam_im * lam_im
    co_re = (num_re * lam_re + num_im * lam_im) / den
    co_im = (num_im * lam_re - num_re * lam_im) / den
    bb_re = co_re[..., None] * b_re - co_im[..., None] * b_im
    bb_im = co_re[..., None] * b_im + co_im[..., None] * b_re
    cp_re = c_re[None] * p_re[:, :, None, :] - c_im[None] * p_im[:, :, None, :]
    cp_im = c_re[None] * p_im[:, :, None, :] + c_im[None] * p_re[:, :, None, :]
    kern = (jnp.einsum('tghn,gnk->tghk', cp_re[:c], bb_re, precision=hi)
            - jnp.einsum('tghn,gnk->tghk', cp_im[:c], bb_im, precision=hi))
    lag = jnp.arange(c)[:, None] - jnp.arange(c)[None, :]
    toep = kern[jnp.clip(lag, 0, c - 1)]
    toep = jnp.where((lag >= 0)[:, :, None, None, None], toep, 0.0)
    m_t = toep.transpose(2, 0, 3, 1, 4).reshape(g, c * h, c * h)
    q_re, q_im = p_re[:c][::-1], p_im[:c][::-1]
    e_re = q_re[:, :, :, None] * bb_re[None] - q_im[:, :, :, None] * bb_im[None]
    e_im = q_re[:, :, :, None] * bb_im[None] + q_im[:, :, :, None] * bb_re[None]
    e_t = jnp.concatenate([e_re, e_im], axis=2).transpose(1, 2, 0, 3).reshape(g, 2 * n, c * h)
    f_re = cp_re[1:].transpose(1, 0, 2, 3).reshape(g, c * h, n)
    f_im = -cp_im[1:].transpose(1, 0, 2, 3).reshape(g, c * h, n)
    f_t = jnp.concatenate([f_re, f_im], axis=2)
    n_steps = max(1, int(math.log2(n_chunks)))
    assert n_steps <= 8
    ar, ai = p_re[c], p_im[c]
    cols_re, cols_im = [], []
    for _ in range(n_steps):
        cols_re.append(ar)
        cols_im.append(ai)
        ar, ai = ar * ar - ai * ai, 2.0 * ar * ai
    pad = [jnp.zeros((g, n), F32)] * (8 - n_steps)
    a_pow = jnp.stack(cols_re + pad + cols_im + pad, axis=2)
    d_col = jnp.tile(d_skip.reshape(g, 1, h), (1, c, 1)).reshape(g, c * h, 1)
    return m_t.astype(BF16), e_t.astype(BF16), f_t.astype(BF16), a_pow, d_col


def _s5_kernel(u_ref, m_ref, e_ref, f_ref, ap_ref, d_ref, y_ref, *, n_chunks):
    u = u_ref[0]
    y = jnp.dot(m_ref[0], u, preferred_element_type=F32)
    s = jnp.dot(e_ref[0], u, preferred_element_type=F32)
    n = s.shape[0] // 2
    cols = s.shape[1]
    chunk = lax.broadcasted_iota(jnp.int32, (n, cols), 1) % n_chunks
    ap = ap_ref[0]
    z_re, z_im = s[:n], s[n:]
    step = 1
    m = 0
    while step < n_chunks:
        keep = chunk >= step
        s_re = jnp.where(keep, pltpu.roll(z_re, step, axis=1), 0.0)
        s_im = jnp.where(keep, pltpu.roll(z_im, step, axis=1), 0.0)
        a_re, a_im = ap[:, m:m + 1], ap[:, 8 + m:9 + m]
        z_re, z_im = z_re + a_re * s_re - a_im * s_im, z_im + a_re * s_im + a_im * s_re
        step *= 2
        m += 1
    first = chunk >= 1
    prev = jnp.concatenate([jnp.where(first, pltpu.roll(z_re, 1, axis=1), 0.0),
                            jnp.where(first, pltpu.roll(z_im, 1, axis=1), 0.0)], axis=0)
    y = y + jnp.dot(f_ref[0], prev.astype(BF16), preferred_element_type=F32)
    y = y + u.astype(F32) * d_ref[0]
    y_ref[0] = y.astype(y_ref.dtype)


def _s5(u_t, mats, n_chunks):
    g, p, r = u_t.shape
    m_t, e_t, f_t, a_pow, d_col = mats
    n2 = e_t.shape[1]
    spec3 = lambda a, b: pl.BlockSpec((1, a, b), lambda i: (i, 0, 0))
    return pl.pallas_call(
        functools.partial(_s5_kernel, n_chunks=n_chunks),
        grid=(g,),
        in_specs=[spec3(p, r), spec3(p, p), spec3(n2, p), spec3(p, n2), spec3(n2 // 2, 16), spec3(p, 1)],
        out_specs=spec3(p, r),
        out_shape=jax.ShapeDtypeStruct((g, p, r), BF16),
        compiler_params=_params(("parallel",)),
    )(u_t, m_t, e_t, f_t, a_pow, d_col)


def _glu_kernel(yt_ref, w_ref, b_ref, nw_ref, g_ref, o_ref, wb_ref):
    @pl.when((pl.program_id(0) == 0) & (pl.program_id(1) == 0))
    def _():
        wb_ref[...] = w_ref[...].astype(BF16)

    groups, gsz, rows = yt_ref.shape
    y = yt_ref[...].reshape(groups * gsz, rows).astype(F32).T
    y = y * (0.5 * (1.0 + jnp.tanh(math.sqrt(2.0 / math.pi) * (y + 0.044715 * (y * y * y)))))
    z = jnp.dot(y.astype(BF16), wb_ref[...], preferred_element_type=F32) + b_ref[...]
    y = y * jax.nn.sigmoid(z)
    ms = jnp.mean(y * y, axis=-1, keepdims=True)
    y = y * lax.rsqrt(ms + EPS) * nw_ref[...]
    g = g_ref[...].astype(F32)
    o_ref[...] = (y * (g * jax.nn.sigmoid(g))).astype(o_ref.dtype)


def _glu(y_t, w_glu, layer, b_glu, norm_w, proj, tr=256):
    groups, _, n_rows = y_t.shape
    d = groups * SSM_GROUP
    tr = min(tr, n_rows)
    n_proj = proj.shape[1]
    proj_slabs = proj.reshape(n_rows, S5_CHUNK * n_proj)
    gate_stride = n_proj // d
    out = pl.pallas_call(
        _glu_kernel,
        grid=(S5_CHUNK, n_rows // tr),
        in_specs=[pl.BlockSpec((groups, SSM_GROUP, tr), lambda i, r: (0, i, r)),
                  _single((None, d, d), lambda i, r: (layer, 0, 0)),
                  pl.BlockSpec((1, d), lambda i, r: (0, 0)),
                  pl.BlockSpec((1, d), lambda i, r: (0, 0)),
                  pl.BlockSpec((tr, d), lambda i, r: (r, i * gate_stride))],
        out_specs=pl.BlockSpec((tr, d), lambda i, r: (r, i)),
        out_shape=jax.ShapeDtypeStruct((n_rows, S5_CHUNK * d), BF16),
        scratch_shapes=[pltpu.VMEM((d, d), BF16)],
        compiler_params=_params(("arbitrary", "arbitrary")),
    )(y_t, w_glu, b_glu.reshape(1, d), norm_w.reshape(1, d), proj_slabs)
    return out.reshape(n_rows * S5_CHUNK, d)


def _retention_tables(seq, head_dim):
    pos = jnp.arange(seq, dtype=F32)
    inv_freq = ROPE_BASE ** (-jnp.arange(0, head_dim, 2, dtype=F32) / head_dim)
    ang = pos[:, None] * inv_freq[None, :]
    cos, sin = jnp.cos(ang), jnp.sin(ang)
    log_gamma = jnp.log1p(-jnp.power(2.0, -5.0 - jnp.arange(RET_HEADS, dtype=F32)))
    idx = jnp.arange(RET_CHUNK, dtype=F32)
    rel = idx[:, None] - idx[None, :]
    causal = rel >= 0
    decay_in = jnp.where(causal[None],
                         jnp.exp(jnp.where(causal, rel, 0.0)[None] * log_gamma[:, None, None]),
                         0.0)
    zeta = jnp.exp((RET_CHUNK - 1.0 - idx)[None, :] * log_gamma[:, None])
    xi = jnp.exp((idx + 1.0)[None, :] * log_gamma[:, None])
    gamma_c = jnp.exp(RET_CHUNK * log_gamma)
    bcast = lambda a: jnp.broadcast_to(a[:, :, None], a.shape + (head_dim,))
    gamma_row = jnp.broadcast_to(gamma_c[:, None, None], (RET_HEADS, 1, head_dim))
    return cos, sin, decay_in, bcast(zeta), bcast(xi), gamma_row


def _retention_kernel(q_ref, k_ref, v_ref, g_ref, cos_ref, sin_ref, dec_ref, zeta_ref, xi_ref,
                      gam_ref, nw_ref, o_ref, r_ref, *, n_chunks, head_dim):
    half = head_dim // 2
    scale = head_dim ** -0.5
    r_ref[...] = jnp.zeros_like(r_ref)

    def rope(t, cos, sin):
        t1, t2 = t[:, :half], t[:, half:]
        return jnp.concatenate([t1 * cos - t2 * sin, t2 * cos + t1 * sin], axis=-1)

    def body(c, carry):
        rows = pl.ds(pl.multiple_of(c * RET_CHUNK, RET_CHUNK), RET_CHUNK)
        cos, sin = cos_ref[rows, :], sin_ref[rows, :]
        q = rope(q_ref[rows, :].astype(F32), cos, sin)
        k = rope(k_ref[rows, :].astype(F32), cos, sin) * scale
        v = v_ref[rows, :]
        qb = q.astype(BF16)
        scores = lax.dot_general(qb, k.astype(BF16), (((1,), (1,)), ((), ())),
                                 preferred_element_type=F32) * dec_ref[0]
        out = jnp.dot(scores.astype(BF16), v, preferred_element_type=F32)
        out = out + jnp.dot(qb, r_ref[...].astype(BF16), preferred_element_type=F32) * xi_ref[0]
        kz = (k * zeta_ref[0]).T.astype(BF16)
        r_ref[...] = r_ref[...] * gam_ref[0] + jnp.dot(kz, v, preferred_element_type=F32)
        ms = jnp.mean(out * out, axis=-1, keepdims=True)
        g = g_ref[rows, :].astype(F32)
        out = out * lax.rsqrt(ms + EPS) * nw_ref[...] * (g * jax.nn.sigmoid(g))
        o_ref[rows, :] = out.astype(o_ref.dtype)
        return carry

    lax.fori_loop(0, n_chunks, body, 0)


def _retention(proj, ret_norm_w, tables, batch, seq, col0):
    t = proj.shape[0]
    d_ret = ret_norm_w.shape[0]
    hd = d_ret // RET_HEADS
    cos, sin, decay_in, zeta, xi, gamma_row = tables
    n_chunks = seq // RET_CHUNK

    def col(group):
        return pl.BlockSpec((seq, hd), lambda b, h: (b, col0 + group * RET_HEADS + h))

    per_head = lambda a, b_: pl.BlockSpec((1, a, b_), lambda b, h: (h, 0, 0))
    const2 = lambda a, b_: pl.BlockSpec((a, b_), lambda b, h: (0, 0))
    return pl.pallas_call(
        functools.partial(_retention_kernel, n_chunks=n_chunks, head_dim=hd),
        grid=(batch, RET_HEADS),
        in_specs=[col(0), col(1), col(2), col(3),
                  const2(seq, hd // 2), const2(seq, hd // 2),
                  per_head(RET_CHUNK, RET_CHUNK), per_head(RET_CHUNK, hd), per_head(RET_CHUNK, hd),
                  per_head(1, hd),
                  pl.BlockSpec((1, hd), lambda b, h: (0, h))],
        out_specs=pl.BlockSpec((seq, hd), lambda b, h: (b, h)),
        out_shape=jax.ShapeDtypeStruct((t, d_ret), BF16),
        scratch_shapes=[pltpu.VMEM((hd, hd), F32)],
        compiler_params=_params(("parallel", "parallel")),
    )(proj, proj, proj, proj, cos, sin, decay_in, zeta, xi, gamma_row, ret_norm_w.reshape(1, d_ret))


def _layer(x2, batch, seq, layer, norm_w, w_in, lam_re, lam_im, b_re, b_im, c_re, c_im, d_skip, log_dt,
           w_glu, b_glu, ssm_norm_w, ret_norm_w, w_out, ret_tables):
    t, d_model = x2.shape
    d_ssm = d_skip.shape[0]
    d_ret = ret_norm_w.shape[0]
    hd = d_ret // RET_HEADS
    n_chunks = seq // S5_CHUNK
    n_rows = batch * n_chunks
    n_proj = w_in.shape[-1]

    h = _rmsnorm(x2, norm_w, BF16)
    proj = _inproj(h, w_in, layer, d_ssm, n_proj - d_ssm)
    w_u_t = w_in[layer, :, :d_ssm].T.astype(BF16)
    u_t = _inproj_t(w_u_t, h, n_rows)
    mats = _s5_matrices(lam_re, lam_im, b_re, b_im, c_re, c_im, d_skip, log_dt, n_chunks)
    y_t = _s5(u_t, mats, n_chunks)
    y_ssm = _glu(y_t, w_glu, layer, b_glu, ssm_norm_w, proj)
    y_ret = _retention(proj, ret_norm_w, ret_tables, batch, seq, d_ssm // hd)
    return _outproj(y_ssm, y_ret, w_out, layer, x2)


def kernel(x, norm_w, w_in, ssm_lambda_re, ssm_lambda_im, ssm_b_re, ssm_b_im, ssm_c_re, ssm_c_im,
           ssm_d, ssm_log_dt, ssm_w_glu, ssm_b_glu, ssm_norm_w, ret_norm_w, w_out, final_norm_w):
    batch, seq, d_model = x.shape
    depth = norm_w.shape[0]
    d_ret = ret_norm_w.shape[1]
    ret_tables = _retention_tables(seq, d_ret // RET_HEADS)
    h = x.reshape(batch * seq, d_model)
    for layer in range(depth):
        h = _layer(h, batch, seq, layer, norm_w[layer], w_in,
                   ssm_lambda_re[layer], ssm_lambda_im[layer], ssm_b_re[layer], ssm_b_im[layer],
                   ssm_c_re[layer], ssm_c_im[layer], ssm_d[layer], ssm_log_dt[layer],
                   ssm_w_glu, ssm_b_glu[layer], ssm_norm_w[layer], ret_norm_w[layer],
                   w_out, ret_tables)
    out = _rmsnorm(h, final_norm_w, F32)
    return out.reshape(batch, seq, d_model)
```

```python
import functools
import math

import jax
import jax.numpy as jnp
from jax import lax
from jax.experimental import pallas as pl
from jax.experimental.pallas import tpu as pltpu

F32 = jnp.float32
BF16 = jnp.bfloat16

EPS = 1e-6
SSM_GROUP = 16
RET_HEADS = 8
RET_CHUNK = 128
ROPE_BASE = 10000.0
S5_CHUNK = 16

VMEM_LIMIT = 56 * 1024 * 1024


def _params(sem):
    return pltpu.CompilerParams(dimension_semantics=sem, vmem_limit_bytes=VMEM_LIMIT)


def _single(block_shape, index_map):
    return pl.BlockSpec(block_shape, index_map, pipeline_mode=pl.Buffered(1))


def _rmsnorm_kernel(x_ref, w_ref, o_ref):
    x = x_ref[...]
    ms = jnp.mean(x * x, axis=-1, keepdims=True)
    o_ref[...] = (x * lax.rsqrt(ms + EPS) * w_ref[...]).astype(o_ref.dtype)


def _rmsnorm(x, w, out_dtype, tm=512):
    t, d = x.shape
    tm = min(tm, t)
    return pl.pallas_call(
        _rmsnorm_kernel,
        grid=(t // tm,),
        in_specs=[pl.BlockSpec((tm, d), lambda i: (i, 0)),
                  pl.BlockSpec((1, d), lambda i: (0, 0))],
        out_specs=pl.BlockSpec((tm, d), lambda i: (i, 0)),
        out_shape=jax.ShapeDtypeStruct((t, d), out_dtype),
        compiler_params=_params(("parallel",)),
    )(x, w.reshape(1, d))


def _rmsnorm_slab_kernel(x_ref, w_ref, p_ref, o_ref, os_ref):
    x = x_ref[...]
    ms = jnp.mean(x * x, axis=-1, keepdims=True)
    y = (x * lax.rsqrt(ms + EPS) * w_ref[...]).astype(BF16)
    o_ref[...] = y
    ys = jnp.dot(p_ref[...], y, preferred_element_type=F32).astype(BF16)
    os_ref[...] = ys.reshape(os_ref.shape)


def _rmsnorm_slab(x, w, tm=512):
    t, d = x.shape
    tm = min(tm, t)
    cpt = tm // S5_CHUNK
    row = jnp.arange(tm)
    perm = (row[None, :] == ((row % cpt) * S5_CHUNK + row // cpt)[:, None]).astype(BF16)
    return pl.pallas_call(
        _rmsnorm_slab_kernel,
        grid=(t // tm,),
        in_specs=[pl.BlockSpec((tm, d), lambda i: (i, 0)),
                  pl.BlockSpec((1, d), lambda i: (0, 0)),
                  pl.BlockSpec((tm, tm), lambda i: (0, 0))],
        out_specs=[pl.BlockSpec((tm, d), lambda i: (i, 0)),
                   pl.BlockSpec((S5_CHUNK, cpt, d), lambda i: (0, i, 0))],
        out_shape=[jax.ShapeDtypeStruct((t, d), BF16),
                   jax.ShapeDtypeStruct((S5_CHUNK, t // S5_CHUNK, d), BF16)],
        compiler_params=_params(("parallel",)),
    )(x, w.reshape(1, d), perm)


def _inproj_kernel(a_ref, w_ref, o_ref, wb_ref):
    @pl.when(pl.program_id(1) == 0)
    def _():
        wb_ref[...] = w_ref[...].astype(BF16)

    o_ref[...] = jnp.dot(a_ref[...], wb_ref[...], preferred_element_type=F32).astype(o_ref.dtype)


def _inproj(h, w_in, layer, col_start, n_cols, tm=1024, tn=1024):
    t, k = h.shape
    tm, tn = min(tm, t), min(tn, n_cols)
    assert col_start % tn == 0
    c0 = col_start // tn
    return pl.pallas_call(
        _inproj_kernel,
        grid=(n_cols // tn, t // tm),
        in_specs=[pl.BlockSpec((tm, k), lambda j, i: (i, 0)),
                  _single((None, k, tn), lambda j, i: (layer, 0, c0 + j))],
        out_specs=pl.BlockSpec((tm, tn), lambda j, i: (i, j)),
        out_shape=jax.ShapeDtypeStruct((t, n_cols), BF16),
        scratch_shapes=[pltpu.VMEM((k, tn), BF16)],
        compiler_params=_params(("arbitrary", "arbitrary")),
    )(h, w_in)


def _inproj_t_kernel(h_ref, w_ref, o_ref, wt_ref):
    @pl.when(pl.program_id(1) == 0)
    def _():
        k, tc = w_ref.shape
        for k0 in range(0, k, tc):
            wt_ref[:, k0:k0 + tc] = w_ref[k0:k0 + tc, :].T.astype(BF16)

    acc = lax.dot_general(wt_ref[...], h_ref[...], (((1,), (1,)), ((), ())),
                          preferred_element_type=F32)
    o_ref[...] = acc.astype(BF16).reshape(o_ref.shape)


def _inproj_t(h_slab, w_in, layer, n_ch, tc=1024):
    n_slabs, n_rows, k = h_slab.shape
    tc = min(tc, n_ch)
    return pl.pallas_call(
        _inproj_t_kernel,
        grid=(n_ch // tc, n_slabs),
        in_specs=[pl.BlockSpec((None, n_rows, k), lambda c, j: (j, 0, 0)),
                  _single((None, k, tc), lambda c, j: (layer, 0, c))],
        out_specs=pl.BlockSpec((tc // SSM_GROUP, SSM_GROUP, n_rows), lambda c, j: (c, j, 0)),
        out_shape=jax.ShapeDtypeStruct((n_ch // SSM_GROUP, n_slabs * SSM_GROUP, n_rows), BF16),
        scratch_shapes=[pltpu.VMEM((tc, k), BF16)],
        compiler_params=_params(("arbitrary", "arbitrary")),
    )(h_slab, w_in)


def _outproj_kernel(ys_ref, yr_ref, ws_ref, wr_ref, x_ref, o_ref, wsb_ref, wrb_ref):
    @pl.when(pl.program_id(1) == 0)
    def _():
        wsb_ref[...] = ws_ref[...].astype(BF16)
        wrb_ref[...] = wr_ref[...].astype(BF16)

    acc = jnp.dot(ys_ref[...], wsb_ref[...], preferred_element_type=F32)
    acc = acc + jnp.dot(yr_ref[...], wrb_ref[...], preferred_element_type=F32)
    o_ref[...] = x_ref[...] + acc


def _outproj(y_ssm, y_ret, w_out, layer, x, tm=1024, tn=512):
    t, ds = y_ssm.shape
    _, dr = y_ret.shape
    assert ds == dr
    n = w_out.shape[-1]
    tm, tn = min(tm, t), min(tn, n)
    return pl.pallas_call(
        _outproj_kernel,
        grid=(n // tn, t // tm),
        in_specs=[pl.BlockSpec((tm, ds), lambda j, i: (i, 0)),
                  pl.BlockSpec((tm, dr), lambda j, i: (i, 0)),
                  _single((None, ds, tn), lambda j, i: (layer, 0, j)),
                  _single((None, dr, tn), lambda j, i: (layer, 1, j)),
                  pl.BlockSpec((tm, tn), lambda j, i: (i, j))],
        out_specs=pl.BlockSpec((tm, tn), lambda j, i: (i, j)),
        out_shape=jax.ShapeDtypeStruct((t, n), F32),
        scratch_shapes=[pltpu.VMEM((ds, tn), BF16), pltpu.VMEM((dr, tn), BF16)],
        compiler_params=_params(("arbitrary", "arbitrary")),
    )(y_ssm, y_ret, w_out, w_out, x)


def _s5_matrices(lam_re, lam_im, b_re, b_im, c_re, c_im, d_skip, log_dt, n_chunks):
    g, n = lam_re.shape
    h = SSM_GROUP
    c = S5_CHUNK
    hi = lax.Precision.HIGHEST
    dt = jnp.exp(log_dt)[:, None]
    tau = jnp.arange(c + 1, dtype=F32)[:, None, None]
    mag = jnp.exp(tau * (lam_re * dt)[None])
    ang = tau * (lam_im * dt)[None]
    p_re, p_im = mag * jnp.cos(ang), mag * jnp.sin(ang)
    ab_re, ab_im = p_re[1], p_im[1]
    num_re, num_im = ab_re - 1.0, ab_im
    den = lam_re * lam_re + lam_im * lam_im
    co_re = (num_re * lam_re + num_im * lam_im) / den
    co_im = (num_im * lam_re - num_re * lam_im) / den
    bb_re = co_re[..., None] * b_re - co_im[..., None] * b_im
    bb_im = co_re[..., None] * b_im + co_im[..., None] * b_re
    cp_re = c_re[None] * p_re[:, :, None, :] - c_im[None] * p_im[:, :, None, :]
    cp_im = c_re[None] * p_im[:, :, None, :] + c_im[None] * p_re[:, :, None, :]
    kern = (jnp.einsum('tghn,gnk->ghtk', cp_re[:c], bb_re, precision=hi)
            - jnp.einsum('tghn,gnk->ghtk', cp_im[:c], bb_im, precision=hi))
    k_rev = kern[:, :, ::-1, :].reshape(g, h, c * h)
    k_pad = jnp.concatenate([k_rev, jnp.zeros((g, h, (c - 1) * h), F32)], axis=-1)
    m_t = jnp.stack([k_pad[:, :, (c - 1 - i) * h:(c - 1 - i) * h + c * h] for i in range(c)], axis=1)
    m_t = m_t.reshape(g, c * h, c * h)
    col = jnp.arange(c * h)
    spread_j = (col[None, :] // h == jnp.arange(c)[:, None]).astype(F32)
    spread_h = (col[None, :] % h == jnp.arange(h)[:, None]).astype(F32)
    q_re = jnp.einsum('jgn,jp->gnp', p_re[:c][::-1], spread_j, precision=hi)
    q_im = jnp.einsum('jgn,jp->gnp', p_im[:c][::-1], spread_j, precision=hi)
    x_re = jnp.einsum('gnk,kp->gnp', bb_re, spread_h, precision=hi)
    x_im = jnp.einsum('gnk,kp->gnp', bb_im, spread_h, precision=hi)
    e_t = jnp.concatenate([q_re * x_re - q_im * x_im, q_re * x_im + q_im * x_re], axis=1)
    f_re = cp_re[1:].transpose(1, 0, 2, 3).reshape(g, c * h, n)
    f_im = -cp_im[1:].transpose(1, 0, 2, 3).reshape(g, c * h, n)
    f_t = jnp.concatenate([f_re, f_im], axis=2)
    n_steps = max(1, int(math.log2(n_chunks)))
    assert n_steps <= 8
    ar, ai = p_re[c], p_im[c]
    cols_re, cols_im = [], []
    for _ in range(n_steps):
        cols_re.append(ar)
        cols_im.append(ai)
        ar, ai = ar * ar - ai * ai, 2.0 * ar * ai
    pad = [jnp.zeros((g, n), F32)] * (8 - n_steps)
    a_pow = jnp.stack(cols_re + pad + cols_im + pad, axis=2)
    d_col = jnp.tile(d_skip.reshape(g, 1, h), (1, c, 1)).reshape(g, c * h, 1)
    return m_t.astype(BF16), e_t.astype(BF16), f_t.astype(BF16), a_pow, d_col


def _s5_kernel(u_ref, m_ref, e_ref, f_ref, ap_ref, d_ref, y_ref, *, n_chunks):
    u = u_ref[0]
    y = jnp.dot(m_ref[0], u, preferred_element_type=F32)
    s = jnp.dot(e_ref[0], u, preferred_element_type=F32)
    n = s.shape[0] // 2
    cols = s.shape[1]
    chunk = lax.broadcasted_iota(jnp.int32, (n, cols), 1) % n_chunks
    ap = ap_ref[0]
    z_re, z_im = s[:n], s[n:]
    step = 1
    m = 0
    while step < n_chunks:
        keep = chunk >= step
        s_re = jnp.where(keep, pltpu.roll(z_re, step, axis=1), 0.0)
        s_im = jnp.where(keep, pltpu.roll(z_im, step, axis=1), 0.0)
        a_re, a_im = ap[:, m:m + 1], ap[:, 8 + m:9 + m]
        z_re, z_im = z_re + a_re * s_re - a_im * s_im, z_im + a_re * s_im + a_im * s_re
        step *= 2
        m += 1
    first = chunk >= 1
    prev = jnp.concatenate([jnp.where(first, pltpu.roll(z_re, 1, axis=1), 0.0),
                            jnp.where(first, pltpu.roll(z_im, 1, axis=1), 0.0)], axis=0)
    y = y + jnp.dot(f_ref[0], prev.astype(BF16), preferred_element_type=F32)
    y = y + u.astype(F32) * d_ref[0]
    y_ref[0] = y.astype(y_ref.dtype)


def _s5(u_t, mats, n_chunks):
    g, p, r = u_t.shape
    m_t, e_t, f_t, a_pow, d_col = mats
    n2 = e_t.shape[1]
    spec3 = lambda a, b: pl.BlockSpec((1, a, b), lambda i: (i, 0, 0))
    return pl.pallas_call(
        functools.partial(_s5_kernel, n_chunks=n_chunks),
        grid=(g,),
        in_specs=[spec3(p, r), spec3(p, p), spec3(n2, p), spec3(p, n2), spec3(n2 // 2, 16), spec3(p, 1)],
        out_specs=spec3(p, r),
        out_shape=jax.ShapeDtypeStruct((g, p, r), BF16),
        compiler_params=_params(("parallel",)),
    )(u_t, m_t, e_t, f_t, a_pow, d_col)


def _glu_kernel(yt_ref, w_ref, b_ref, nw_ref, g_ref, o_ref, wb_ref):
    @pl.when((pl.program_id(0) == 0) & (pl.program_id(1) == 0))
    def _():
        wb_ref[...] = w_ref[...].astype(BF16)

    groups, gsz, rows = yt_ref.shape
    y = yt_ref[...].reshape(groups * gsz, rows).astype(F32).T
    y = y * (0.5 * (1.0 + jnp.tanh(math.sqrt(2.0 / math.pi) * (y + 0.044715 * (y * y * y)))))
    z = jnp.dot(y.astype(BF16), wb_ref[...], preferred_element_type=F32) + b_ref[...]
    y = y * jax.nn.sigmoid(z)
    ms = jnp.mean(y * y, axis=-1, keepdims=True)
    y = y * lax.rsqrt(ms + EPS) * nw_ref[...]
    g = g_ref[...].astype(F32)
    o_ref[...] = (y * (g * jax.nn.sigmoid(g))).astype(o_ref.dtype)


def _glu(y_t, w_glu, layer, b_glu, norm_w, gate_slab, tr=256):
    groups, _, n_rows = y_t.shape
    d = groups * SSM_GROUP
    tr = min(tr, n_rows)
    out = pl.pallas_call(
        _glu_kernel,
        grid=(S5_CHUNK, n_rows // tr),
        in_specs=[pl.BlockSpec((groups, SSM_GROUP, tr), lambda i, r: (0, i, r)),
                  _single((None, d, d), lambda i, r: (layer, 0, 0)),
                  pl.BlockSpec((1, d), lambda i, r: (0, 0)),
                  pl.BlockSpec((1, d), lambda i, r: (0, 0)),
                  pl.BlockSpec((None, tr, d), lambda i, r: (i, r, 0))],
        out_specs=pl.BlockSpec((tr, d), lambda i, r: (r, i)),
        out_shape=jax.ShapeDtypeStruct((n_rows, S5_CHUNK * d), BF16),
        scratch_shapes=[pltpu.VMEM((d, d), BF16)],
        compiler_params=_params(("arbitrary", "arbitrary")),
    )(y_t, w_glu, b_glu.reshape(1, d), norm_w.reshape(1, d), gate_slab)
    return out.reshape(n_rows * S5_CHUNK, d)


def _retention_tables(seq, head_dim):
    pos = jnp.arange(seq, dtype=F32)
    inv_freq = ROPE_BASE ** (-jnp.arange(0, head_dim, 2, dtype=F32) / head_dim)
    ang = pos[:, None] * inv_freq[None, :]
    cos, sin = jnp.cos(ang), jnp.sin(ang)
    log_gamma = jnp.log1p(-jnp.power(2.0, -5.0 - jnp.arange(RET_HEADS, dtype=F32)))
    idx = jnp.arange(RET_CHUNK, dtype=F32)
    rel = idx[:, None] - idx[None, :]
    causal = rel >= 0
    decay_in = jnp.where(causal[None],
                         jnp.exp(jnp.where(causal, rel, 0.0)[None] * log_gamma[:, None, None]),
                         0.0)
    zeta = jnp.exp((RET_CHUNK - 1.0 - idx)[None, :] * log_gamma[:, None])
    xi = jnp.exp((idx + 1.0)[None, :] * log_gamma[:, None])
    gamma_c = jnp.exp(RET_CHUNK * log_gamma)
    bcast = lambda a: jnp.broadcast_to(a[:, :, None], a.shape + (head_dim,))
    gamma_row = jnp.broadcast_to(gamma_c[:, None, None], (RET_HEADS, 1, head_dim))
    return cos, sin, decay_in, bcast(zeta), bcast(xi), gamma_row


def _retention_kernel(q_ref, k_ref, v_ref, g_ref, cos_ref, sin_ref, dec_ref, zeta_ref, xi_ref,
                      gam_ref, nw_ref, o_ref, r_ref, *, n_chunks, head_dim):
    half = head_dim // 2
    scale = head_dim ** -0.5
    r_ref[...] = jnp.zeros_like(r_ref)

    def rope(t, cos, sin):
        t1, t2 = t[:, :half], t[:, half:]
        return jnp.concatenate([t1 * cos - t2 * sin, t2 * cos + t1 * sin], axis=-1)

    def body(c, carry):
        rows = pl.ds(pl.multiple_of(c * RET_CHUNK, RET_CHUNK), RET_CHUNK)
        cos, sin = cos_ref[rows, :], sin_ref[rows, :]
        q = rope(q_ref[rows, :].astype(F32), cos, sin)
        k = rope(k_ref[rows, :].astype(F32), cos, sin) * scale
        v = v_ref[rows, :]
        qb = q.astype(BF16)
        scores = lax.dot_general(qb, k.astype(BF16), (((1,), (1,)), ((), ())),
                                 preferred_element_type=F32) * dec_ref[0]
        out = jnp.dot(scores.astype(BF16), v, preferred_element_type=F32)
        out = out + jnp.dot(qb, r_ref[...].astype(BF16), preferred_element_type=F32) * xi_ref[0]
        kz = (k * zeta_ref[0]).T.astype(BF16)
        r_ref[...] = r_ref[...] * gam_ref[0] + jnp.dot(kz, v, preferred_element_type=F32)
        ms = jnp.mean(out * out, axis=-1, keepdims=True)
        g = g_ref[rows, :].astype(F32)
        out = out * lax.rsqrt(ms + EPS) * nw_ref[...] * (g * jax.nn.sigmoid(g))
        o_ref[rows, :] = out.astype(o_ref.dtype)
        return carry

    lax.fori_loop(0, n_chunks, body, 0)


def _retention(proj, ret_norm_w, tables, batch, seq, col0):
    t = proj.shape[0]
    d_ret = ret_norm_w.shape[0]
    hd = d_ret // RET_HEADS
    cos, sin, decay_in, zeta, xi, gamma_row = tables
    n_chunks = seq // RET_CHUNK

    def col(group):
        return pl.BlockSpec((seq, hd), lambda b, h: (b, col0 + group * RET_HEADS + h))

    per_head = lambda a, b_: pl.BlockSpec((1, a, b_), lambda b, h: (h, 0, 0))
    const2 = lambda a, b_: pl.BlockSpec((a, b_), lambda b, h: (0, 0))
    return pl.pallas_call(
        functools.partial(_retention_kernel, n_chunks=n_chunks, head_dim=hd),
        grid=(batch, RET_HEADS),
        in_specs=[col(0), col(1), col(2), col(3),
                  const2(seq, hd // 2), const2(seq, hd // 2),
                  per_head(RET_CHUNK, RET_CHUNK), per_head(RET_CHUNK, hd), per_head(RET_CHUNK, hd),
                  per_head(1, hd),
                  pl.BlockSpec((1, hd), lambda b, h: (0, h))],
        out_specs=pl.BlockSpec((seq, hd), lambda b, h: (b, h)),
        out_shape=jax.ShapeDtypeStruct((t, d_ret), BF16),
        scratch_shapes=[pltpu.VMEM((hd, hd), F32)],
        compiler_params=_params(("parallel", "parallel")),
    )(proj, proj, proj, proj, cos, sin, decay_in, zeta, xi, gamma_row, ret_norm_w.reshape(1, d_ret))


def _layer(x2, batch, seq, layer, norm_w, w_in, lam_re, lam_im, b_re, b_im, c_re, c_im, d_skip, log_dt,
           w_glu, b_glu, ssm_norm_w, ret_norm_w, w_out, ret_tables):
    t, d_model = x2.shape
    d_ssm = d_skip.shape[0]
    d_ret = ret_norm_w.shape[0]
    hd = d_ret // RET_HEADS
    n_chunks = seq // S5_CHUNK
    n_rows = batch * n_chunks
    n_proj = w_in.shape[-1]

    h, h_slab = _rmsnorm_slab(x2, norm_w)
    proj = _inproj(h, w_in, layer, 2 * d_ssm, n_proj - 2 * d_ssm)
    gate_slab = _inproj(h_slab.reshape(t, d_model), w_in, layer, d_ssm, d_ssm)
    gate_slab = gate_slab.reshape(S5_CHUNK, n_rows, d_ssm)
    u_t = _inproj_t(h_slab, w_in, layer, d_ssm)
    mats = _s5_matrices(lam_re, lam_im, b_re, b_im, c_re, c_im, d_skip, log_dt, n_chunks)
    y_t = _s5(u_t, mats, n_chunks)
    y_ssm = _glu(y_t, w_glu, layer, b_glu, ssm_norm_w, gate_slab)
    y_ret = _retention(proj, ret_norm_w, ret_tables, batch, seq, 0)
    return _outproj(y_ssm, y_ret, w_out, layer, x2)


def kernel(x, norm_w, w_in, ssm_lambda_re, ssm_lambda_im, ssm_b_re, ssm_b_im, ssm_c_re, ssm_c_im,
           ssm_d, ssm_log_dt, ssm_w_glu, ssm_b_glu, ssm_norm_w, ret_norm_w, w_out, final_norm_w):
    batch, seq, d_model = x.shape
    depth = norm_w.shape[0]
    d_ret = ret_norm_w.shape[1]
    ret_tables = _retention_tables(seq, d_ret // RET_HEADS)
    h = x.reshape(batch * seq, d_model)
    for layer in range(depth):
        h = _layer(h, batch, seq, layer, norm_w[layer], w_in,
                   ssm_lambda_re[layer], ssm_lambda_im[layer], ssm_b_re[layer], ssm_b_im[layer],
                   ssm_c_re[layer], ssm_c_im[layer], ssm_d[layer], ssm_log_dt[layer],
                   ssm_w_glu, ssm_b_glu[layer], ssm_norm_w[layer], ret_norm_w[layer],
                   w_out, ret_tables)
    out = _rmsnorm(h, final_norm_w, F32)
    return out.reshape(batch, seq, d_model)
```

```python
import functools
import math

import jax
import jax.numpy as jnp
from jax import lax
from jax.experimental import pallas as pl
from jax.experimental.pallas import tpu as pltpu

F32 = jnp.float32
BF16 = jnp.bfloat16

EPS = 1e-6
SSM_GROUP = 16
RET_HEADS = 8
RET_CHUNK = 128
ROPE_BASE = 10000.0
S5_CHUNK = 16

VMEM_LIMIT = 56 * 1024 * 1024


def _params(sem):
    return pltpu.CompilerParams(dimension_semantics=sem, vmem_limit_bytes=VMEM_LIMIT)


def _single(block_shape, index_map):
    return pl.BlockSpec(block_shape, index_map, pipeline_mode=pl.Buffered(1))


def _rmsnorm_kernel(x_ref, w_ref, o_ref):
    x = x_ref[...]
    ms = jnp.mean(x * x, axis=-1, keepdims=True)
    o_ref[...] = (x * lax.rsqrt(ms + EPS) * w_ref[...]).astype(o_ref.dtype)


def _rmsnorm(x, w, out_dtype, tm=512):
    t, d = x.shape
    tm = min(tm, t)
    return pl.pallas_call(
        _rmsnorm_kernel,
        grid=(t // tm,),
        in_specs=[pl.BlockSpec((tm, d), lambda i: (i, 0)),
                  pl.BlockSpec((1, d), lambda i: (0, 0))],
        out_specs=pl.BlockSpec((tm, d), lambda i: (i, 0)),
        out_shape=jax.ShapeDtypeStruct((t, d), out_dtype),
        compiler_params=_params(("parallel",)),
    )(x, w.reshape(1, d))


def _rmsnorm_slab_kernel(x_ref, w_ref, p_ref, o_ref, os_ref):
    x = x_ref[...]
    ms = jnp.mean(x * x, axis=-1, keepdims=True)
    y = (x * lax.rsqrt(ms + EPS) * w_ref[...]).astype(BF16)
    o_ref[...] = y
    ys = jnp.dot(p_ref[...], y, preferred_element_type=F32).astype(BF16)
    os_ref[...] = ys.reshape(os_ref.shape)


def _rmsnorm_slab(x, w, tm=512):
    t, d = x.shape
    tm = min(tm, t)
    cpt = tm // S5_CHUNK
    row = jnp.arange(tm)
    perm = (row[None, :] == ((row % cpt) * S5_CHUNK + row // cpt)[:, None]).astype(BF16)
    return pl.pallas_call(
        _rmsnorm_slab_kernel,
        grid=(t // tm,),
        in_specs=[pl.BlockSpec((tm, d), lambda i: (i, 0)),
                  pl.BlockSpec((1, d), lambda i: (0, 0)),
                  pl.BlockSpec((tm, tm), lambda i: (0, 0))],
        out_specs=[pl.BlockSpec((tm, d), lambda i: (i, 0)),
                   pl.BlockSpec((S5_CHUNK, cpt, d), lambda i: (0, i, 0))],
        out_shape=[jax.ShapeDtypeStruct((t, d), BF16),
                   jax.ShapeDtypeStruct((S5_CHUNK, t // S5_CHUNK, d), BF16)],
        compiler_params=_params(("parallel",)),
    )(x, w.reshape(1, d), perm)


def _inproj_kernel(a_ref, w_ref, o_ref, wb_ref):
    @pl.when(pl.program_id(1) == 0)
    def _():
        wb_ref[...] = w_ref[...].astype(BF16)

    o_ref[...] = jnp.dot(a_ref[...], wb_ref[...], preferred_element_type=F32).astype(o_ref.dtype)


def _inproj(h, w_in, layer, col_start, n_cols, tm=1024, tn=1024):
    t, k = h.shape
    tm, tn = min(tm, t), min(tn, n_cols)
    assert col_start % tn == 0
    c0 = col_start // tn
    return pl.pallas_call(
        _inproj_kernel,
        grid=(n_cols // tn, t // tm),
        in_specs=[pl.BlockSpec((tm, k), lambda j, i: (i, 0)),
                  _single((None, k, tn), lambda j, i: (layer, 0, c0 + j))],
        out_specs=pl.BlockSpec((tm, tn), lambda j, i: (i, j)),
        out_shape=jax.ShapeDtypeStruct((t, n_cols), BF16),
        scratch_shapes=[pltpu.VMEM((k, tn), BF16)],
        compiler_params=_params(("arbitrary", "arbitrary")),
    )(h, w_in)


def _inproj_t_kernel(h_ref, w_ref, o_ref, wt_ref):
    @pl.when(pl.program_id(1) == 0)
    def _():
        k, tc = w_ref.shape
        for k0 in range(0, k, tc):
            wt_ref[:, k0:k0 + tc] = w_ref[k0:k0 + tc, :].T.astype(BF16)

    acc = lax.dot_general(wt_ref[...], h_ref[...], (((1,), (1,)), ((), ())),
                          preferred_element_type=F32)
    o_ref[...] = acc.astype(BF16).reshape(o_ref.shape)


def _inproj_t(h_slab, w_in, layer, n_ch, tc=1024):
    n_slabs, n_rows, k = h_slab.shape
    tc = min(tc, n_ch)
    return pl.pallas_call(
        _inproj_t_kernel,
        grid=(n_ch // tc, n_slabs),
        in_specs=[pl.BlockSpec((None, n_rows, k), lambda c, j: (j, 0, 0)),
                  _single((None, k, tc), lambda c, j: (layer, 0, c))],
        out_specs=pl.BlockSpec((tc // SSM_GROUP, SSM_GROUP, n_rows), lambda c, j: (c, j, 0)),
        out_shape=jax.ShapeDtypeStruct((n_ch // SSM_GROUP, n_slabs * SSM_GROUP, n_rows), BF16),
        scratch_shapes=[pltpu.VMEM((tc, k), BF16)],
        compiler_params=_params(("arbitrary", "arbitrary")),
    )(h_slab, w_in)


def _outproj_kernel(ys_ref, yr_ref, ws_ref, wr_ref, x_ref, o_ref, wsb_ref, wrb_ref):
    @pl.when(pl.program_id(1) == 0)
    def _():
        wsb_ref[...] = ws_ref[...].astype(BF16)
        wrb_ref[...] = wr_ref[...].astype(BF16)

    acc = jnp.dot(ys_ref[...], wsb_ref[...], preferred_element_type=F32)
    acc = acc + jnp.dot(yr_ref[...], wrb_ref[...], preferred_element_type=F32)
    o_ref[...] = x_ref[...] + acc


def _outproj(y_ssm, y_ret, w_out, layer, x, tm=1024, tn=512):
    t, ds = y_ssm.shape
    _, dr = y_ret.shape
    assert ds == dr
    n = w_out.shape[-1]
    tm, tn = min(tm, t), min(tn, n)
    return pl.pallas_call(
        _outproj_kernel,
        grid=(n // tn, t // tm),
        in_specs=[pl.BlockSpec((tm, ds), lambda j, i: (i, 0)),
                  pl.BlockSpec((tm, dr), lambda j, i: (i, 0)),
                  _single((None, ds, tn), lambda j, i: (layer, 0, j)),
                  _single((None, dr, tn), lambda j, i: (layer, 1, j)),
                  pl.BlockSpec((tm, tn), lambda j, i: (i, j))],
        out_specs=pl.BlockSpec((tm, tn), lambda j, i: (i, j)),
        out_shape=jax.ShapeDtypeStruct((t, n), F32),
        scratch_shapes=[pltpu.VMEM((ds, tn), BF16), pltpu.VMEM((dr, tn), BF16)],
        compiler_params=_params(("arbitrary", "arbitrary")),
    )(y_ssm, y_ret, w_out, w_out, x)


def _s5_matrices(lam_re, lam_im, b_re, b_im, c_re, c_im, d_skip, log_dt, n_chunks):
    g, n = lam_re.shape
    h = SSM_GROUP
    c = S5_CHUNK
    hi = lax.Precision.HIGHEST
    dt = jnp.exp(log_dt)[:, None]
    tau = jnp.arange(c + 1, dtype=F32)[:, None, None]
    mag = jnp.exp(tau * (lam_re * dt)[None])
    ang = tau * (lam_im * dt)[None]
    p_re, p_im = mag * jnp.cos(ang), mag * jnp.sin(ang)
    ab_re, ab_im = p_re[1], p_im[1]
    num_re, num_im = ab_re - 1.0, ab_im
    den = lam_re * lam_re + lam_im * lam_im
    co_re = (num_re * lam_re + num_im * lam_im) / den
    co_im = (num_im * lam_re - num_re * lam_im) / den
    bb_re = co_re[..., None] * b_re - co_im[..., None] * b_im
    bb_im = co_re[..., None] * b_im + co_im[..., None] * b_re
    col = jnp.arange(c * h)
    spread_j = (col[None, :] // h == jnp.arange(c)[:, None]).astype(F32)
    spread_h = (col[None, :] % h == jnp.arange(h)[:, None]).astype(F32)
    q_re = jnp.einsum('jgn,jp->gnp', p_re[:c][::-1], spread_j, precision=hi)
    q_im = jnp.einsum('jgn,jp->gnp', p_im[:c][::-1], spread_j, precision=hi)
    x_re = jnp.einsum('gnk,kp->gnp', bb_re, spread_h, precision=hi)
    x_im = jnp.einsum('gnk,kp->gnp', bb_im, spread_h, precision=hi)
    e_t = jnp.concatenate([q_re * x_re - q_im * x_im, q_re * x_im + q_im * x_re], axis=1)
    c_a = jnp.concatenate([c_re, -c_im], axis=-1)
    c_b = jnp.concatenate([-c_im, -c_re], axis=-1)
    pt_re = p_re[1:].transpose(1, 0, 2)
    pt_im = p_im[1:].transpose(1, 0, 2)
    p_tab = jnp.concatenate([jnp.concatenate([pt_re, pt_re], axis=-1),
                             jnp.concatenate([pt_im, pt_im], axis=-1)], axis=1)
    n_steps = max(1, int(math.log2(n_chunks)))
    assert n_steps <= 8
    ar, ai = p_re[c], p_im[c]
    cols_re, cols_im = [], []
    for _ in range(n_steps):
        cols_re.append(ar)
        cols_im.append(ai)
        ar, ai = ar * ar - ai * ai, 2.0 * ar * ai
    pad = [jnp.zeros((g, n), F32)] * (8 - n_steps)
    a_pow = jnp.stack(cols_re + pad + cols_im + pad, axis=2)
    d_col = jnp.tile(d_skip.reshape(g, 1, h), (1, c, 1)).reshape(g, c * h, 1)
    return e_t.astype(BF16), c_a, c_b, p_tab, a_pow, d_col


def _s5_kernel(u_ref, e_ref, ca_ref, cb_ref, pt_ref, ap_ref, d_ref, y_ref, *, n_chunks):
    u = u_ref[0]
    e = e_ref[0]
    c_a, c_b, p_tab = ca_ref[0], cb_ref[0], pt_ref[0]
    gsz = c_a.shape[0]
    p = e.shape[1]
    n_pos = p // gsz
    k_rev = jnp.dot(c_a.astype(BF16), e, preferred_element_type=F32)
    lane = lax.broadcasted_iota(jnp.int32, k_rev.shape, 1)
    blocks = []
    for i in range(n_pos):
        shift = (n_pos - 1 - i) * gsz
        if shift == 0:
            blocks.append(k_rev)
        else:
            blocks.append(jnp.where(lane < p - shift, pltpu.roll(k_rev, p - shift, axis=1), 0.0))
    m_t = jnp.concatenate(blocks, axis=0).astype(BF16)
    f_t = jnp.concatenate([c_a * p_tab[i:i + 1] + c_b * p_tab[n_pos + i:n_pos + i + 1]
                           for i in range(n_pos)], axis=0).astype(BF16)

    y = jnp.dot(m_t, u, preferred_element_type=F32)
    s = jnp.dot(e, u, preferred_element_type=F32)
    n = s.shape[0] // 2
    cols = s.shape[1]
    chunk = lax.broadcasted_iota(jnp.int32, (n, cols), 1) % n_chunks
    ap = ap_ref[0]
    z_re, z_im = s[:n], s[n:]
    step = 1
    m = 0
    while step < n_chunks:
        keep = chunk >= step
        s_re = jnp.where(keep, pltpu.roll(z_re, step, axis=1), 0.0)
        s_im = jnp.where(keep, pltpu.roll(z_im, step, axis=1), 0.0)
        a_re, a_im = ap[:, m:m + 1], ap[:, 8 + m:9 + m]
        z_re, z_im = z_re + a_re * s_re - a_im * s_im, z_im + a_re * s_im + a_im * s_re
        step *= 2
        m += 1
    first = chunk >= 1
    prev = jnp.concatenate([jnp.where(first, pltpu.roll(z_re, 1, axis=1), 0.0),
                            jnp.where(first, pltpu.roll(z_im, 1, axis=1), 0.0)], axis=0)
    y = y + jnp.dot(f_t, prev.astype(BF16), preferred_element_type=F32)
    y = y + u.astype(F32) * d_ref[0]
    y_ref[0] = y.astype(y_ref.dtype)


def _s5(u_t, mats, n_chunks):
    g, p, r = u_t.shape
    e_t, c_a, c_b, p_tab, a_pow, d_col = mats
    n2 = e_t.shape[1]
    spec3 = lambda a, b: pl.BlockSpec((1, a, b), lambda i: (i, 0, 0))
    return pl.pallas_call(
        functools.partial(_s5_kernel, n_chunks=n_chunks),
        grid=(g,),
        in_specs=[spec3(p, r), spec3(n2, p), spec3(SSM_GROUP, n2), spec3(SSM_GROUP, n2),
                  spec3(2 * S5_CHUNK, n2), spec3(n2 // 2, 16), spec3(p, 1)],
        out_specs=spec3(p, r),
        out_shape=jax.ShapeDtypeStruct((g, p, r), BF16),
        compiler_params=_params(("parallel",)),
    )(u_t, e_t, c_a, c_b, p_tab, a_pow, d_col)


def _glu_kernel(yt_ref, w_ref, b_ref, nw_ref, g_ref, o_ref, wb_ref):
    @pl.when((pl.program_id(0) == 0) & (pl.program_id(1) == 0))
    def _():
        wb_ref[...] = w_ref[...].astype(BF16)

    groups, gsz, rows = yt_ref.shape
    y = yt_ref[...].reshape(groups * gsz, rows).astype(F32).T
    y = y * (0.5 * (1.0 + jnp.tanh(math.sqrt(2.0 / math.pi) * (y + 0.044715 * (y * y * y)))))
    z = jnp.dot(y.astype(BF16), wb_ref[...], preferred_element_type=F32) + b_ref[...]
    y = y * jax.nn.sigmoid(z)
    ms = jnp.mean(y * y, axis=-1, keepdims=True)
    y = y * lax.rsqrt(ms + EPS) * nw_ref[...]
    g = g_ref[...].astype(F32)
    o_ref[...] = (y * (g * jax.nn.sigmoid(g))).astype(o_ref.dtype)


def _glu(y_t, w_glu, layer, b_glu, norm_w, gate_slab, tr=256):
    groups, _, n_rows = y_t.shape
    d = groups * SSM_GROUP
    tr = min(tr, n_rows)
    out = pl.pallas_call(
        _glu_kernel,
        grid=(S5_CHUNK, n_rows // tr),
        in_specs=[pl.BlockSpec((groups, SSM_GROUP, tr), lambda i, r: (0, i, r)),
                  _single((None, d, d), lambda i, r: (layer, 0, 0)),
                  pl.BlockSpec((1, d), lambda i, r: (0, 0)),
                  pl.BlockSpec((1, d), lambda i, r: (0, 0)),
                  pl.BlockSpec((None, tr, d), lambda i, r: (i, r, 0))],
        out_specs=pl.BlockSpec((tr, d), lambda i, r: (r, i)),
        out_shape=jax.ShapeDtypeStruct((n_rows, S5_CHUNK * d), BF16),
        scratch_shapes=[pltpu.VMEM((d, d), BF16)],
        compiler_params=_params(("arbitrary", "arbitrary")),
    )(y_t, w_glu, b_glu.reshape(1, d), norm_w.reshape(1, d), gate_slab)
    return out.reshape(n_rows * S5_CHUNK, d)


def _retention_tables(seq, head_dim):
    pos = jnp.arange(seq, dtype=F32)
    inv_freq = ROPE_BASE ** (-jnp.arange(0, head_dim, 2, dtype=F32) / head_dim)
    ang = pos[:, None] * inv_freq[None, :]
    cos, sin = jnp.cos(ang), jnp.sin(ang)
    log_gamma = jnp.log1p(-jnp.power(2.0, -5.0 - jnp.arange(RET_HEADS, dtype=F32)))
    idx = jnp.arange(RET_CHUNK, dtype=F32)
    rel = idx[:, None] - idx[None, :]
    causal = rel >= 0
    decay_in = jnp.where(causal[None],
                         jnp.exp(jnp.where(causal, rel, 0.0)[None] * log_gamma[:, None, None]),
                         0.0)
    zeta = jnp.exp((RET_CHUNK - 1.0 - idx)[None, :] * log_gamma[:, None])
    xi = jnp.exp((idx + 1.0)[None, :] * log_gamma[:, None])
    gamma_c = jnp.exp(RET_CHUNK * log_gamma)
    bcast = lambda a: jnp.broadcast_to(a[:, :, None], a.shape + (head_dim,))
    gamma_row = jnp.broadcast_to(gamma_c[:, None, None], (RET_HEADS, 1, head_dim))
    return cos, sin, decay_in, bcast(zeta), bcast(xi), gamma_row


def _retention_kernel(q_ref, k_ref, v_ref, g_ref, cos_ref, sin_ref, dec_ref, zeta_ref, xi_ref,
                      gam_ref, nw_ref, o_ref, r_ref, *, n_chunks, head_dim):
    half = head_dim // 2
    scale = head_dim ** -0.5
    r_ref[...] = jnp.zeros_like(r_ref)

    def rope(t, cos, sin):
        t1, t2 = t[:, :half], t[:, half:]
        return jnp.concatenate([t1 * cos - t2 * sin, t2 * cos + t1 * sin], axis=-1)

    def body(c, carry):
        rows = pl.ds(pl.multiple_of(c * RET_CHUNK, RET_CHUNK), RET_CHUNK)
        cos, sin = cos_ref[rows, :], sin_ref[rows, :]
        q = rope(q_ref[rows, :].astype(F32), cos, sin)
        k = rope(k_ref[rows, :].astype(F32), cos, sin) * scale
        v = v_ref[rows, :]
        qb = q.astype(BF16)
        scores = lax.dot_general(qb, k.astype(BF16), (((1,), (1,)), ((), ())),
                                 preferred_element_type=F32) * dec_ref[0]
        out = jnp.dot(scores.astype(BF16), v, preferred_element_type=F32)
        out = out + jnp.dot(qb, r_ref[...].astype(BF16), preferred_element_type=F32) * xi_ref[0]
        kz = (k * zeta_ref[0]).T.astype(BF16)
        r_ref[...] = r_ref[...] * gam_ref[0] + jnp.dot(kz, v, preferred_element_type=F32)
        ms = jnp.mean(out * out, axis=-1, keepdims=True)
        g = g_ref[rows, :].astype(F32)
        out = out * lax.rsqrt(ms + EPS) * nw_ref[...] * (g * jax.nn.sigmoid(g))
        o_ref[rows, :] = out.astype(o_ref.dtype)
        return carry

    lax.fori_loop(0, n_chunks, body, 0, unroll=2)


def _retention(proj, ret_norm_w, tables, batch, seq, col0):
    t = proj.shape[0]
    d_ret = ret_norm_w.shape[0]
    hd = d_ret // RET_HEADS
    cos, sin, decay_in, zeta, xi, gamma_row = tables
    n_chunks = seq // RET_CHUNK

    def col(group):
        return pl.BlockSpec((seq, hd), lambda b, h: (b, col0 + group * RET_HEADS + h))

    per_head = lambda a, b_: pl.BlockSpec((1, a, b_), lambda b, h: (h, 0, 0))
    const2 = lambda a, b_: pl.BlockSpec((a, b_), lambda b, h: (0, 0))
    return pl.pallas_call(
        functools.partial(_retention_kernel, n_chunks=n_chunks, head_dim=hd),
        grid=(batch, RET_HEADS),
        in_specs=[col(0), col(1), col(2), col(3),
                  const2(seq, hd // 2), const2(seq, hd // 2),
                  per_head(RET_CHUNK, RET_CHUNK), per_head(RET_CHUNK, hd), per_head(RET_CHUNK, hd),
                  per_head(1, hd),
                  pl.BlockSpec((1, hd), lambda b, h: (0, h))],
        out_specs=pl.BlockSpec((seq, hd), lambda b, h: (b, h)),
        out_shape=jax.ShapeDtypeStruct((t, d_ret), BF16),
        scratch_shapes=[pltpu.VMEM((hd, hd), F32)],
        compiler_params=_params(("parallel", "parallel")),
    )(proj, proj, proj, proj, cos, sin, decay_in, zeta, xi, gamma_row, ret_norm_w.reshape(1, d_ret))


def _layer(x2, batch, seq, layer, norm_w, w_in, lam_re, lam_im, b_re, b_im, c_re, c_im, d_skip, log_dt,
           w_glu, b_glu, ssm_norm_w, ret_norm_w, w_out, ret_tables):
    t, d_model = x2.shape
    d_ssm = d_skip.shape[0]
    n_chunks = seq // S5_CHUNK
    n_rows = batch * n_chunks
    n_proj = w_in.shape[-1]

    h, h_slab = _rmsnorm_slab(x2, norm_w)
    proj = _inproj(h, w_in, layer, 2 * d_ssm, n_proj - 2 * d_ssm)
    gate_slab = _inproj(h_slab.reshape(t, d_model), w_in, layer, d_ssm, d_ssm)
    gate_slab = gate_slab.reshape(S5_CHUNK, n_rows, d_ssm)
    u_t = _inproj_t(h_slab, w_in, layer, d_ssm)
    mats = _s5_matrices(lam_re, lam_im, b_re, b_im, c_re, c_im, d_skip, log_dt, n_chunks)
    y_t = _s5(u_t, mats, n_chunks)
    y_ssm = _glu(y_t, w_glu, layer, b_glu, ssm_norm_w, gate_slab)
    y_ret = _retention(proj, ret_norm_w, ret_tables, batch, seq, 0)
    return _outproj(y_ssm, y_ret, w_out, layer, x2)


def kernel(x, norm_w, w_in, ssm_lambda_re, ssm_lambda_im, ssm_b_re, ssm_b_im, ssm_c_re, ssm_c_im,
           ssm_d, ssm_log_dt, ssm_w_glu, ssm_b_glu, ssm_norm_w, ret_norm_w, w_out, final_norm_w):
    batch, seq, d_model = x.shape
    depth = norm_w.shape[0]
    d_ret = ret_norm_w.shape[1]
    ret_tables = _retention_tables(seq, d_ret // RET_HEADS)
    h = x.reshape(batch * seq, d_model)
    for layer in range(depth):
        h = _layer(h, batch, seq, layer, norm_w[layer], w_in,
                   ssm_lambda_re[layer], ssm_lambda_im[layer], ssm_b_re[layer], ssm_b_im[layer],
                   ssm_c_re[layer], ssm_c_im[layer], ssm_d[layer], ssm_log_dt[layer],
                   ssm_w_glu, ssm_b_glu[layer], ssm_norm_w[layer], ret_norm_w[layer],
                   w_out, ret_tables)
    out = _rmsnorm(h, final_norm_w, F32)
    return out.reshape(batch, seq, d_model)
```

```python
import functools
import math

import jax
import jax.numpy as jnp
from jax import lax
from jax.experimental import pallas as pl
from jax.experimental.pallas import tpu as pltpu

F32 = jnp.float32
BF16 = jnp.bfloat16

EPS = 1e-6
SSM_GROUP = 16
RET_HEADS = 8
RET_CHUNK = 128
ROPE_BASE = 10000.0
S5_CHUNK = 16

VMEM_LIMIT = 56 * 1024 * 1024


def _params(sem):
    return pltpu.CompilerParams(dimension_semantics=sem, vmem_limit_bytes=VMEM_LIMIT)


def _single(block_shape, index_map):
    return pl.BlockSpec(block_shape, index_map, pipeline_mode=pl.Buffered(1))


def _rmsnorm_kernel(x_ref, w_ref, o_ref):
    x = x_ref[...]
    ms = jnp.mean(x * x, axis=-1, keepdims=True)
    o_ref[...] = (x * lax.rsqrt(ms + EPS) * w_ref[...]).astype(o_ref.dtype)


def _rmsnorm(x, w, out_dtype, tm=512):
    t, d = x.shape
    tm = min(tm, t)
    return pl.pallas_call(
        _rmsnorm_kernel,
        grid=(t // tm,),
        in_specs=[pl.BlockSpec((tm, d), lambda i: (i, 0)),
                  pl.BlockSpec((1, d), lambda i: (0, 0))],
        out_specs=pl.BlockSpec((tm, d), lambda i: (i, 0)),
        out_shape=jax.ShapeDtypeStruct((t, d), out_dtype),
        compiler_params=_params(("parallel",)),
    )(x, w.reshape(1, d))


def _rmsnorm_slab_kernel(x_ref, w_ref, p_ref, o_ref, os_ref):
    x = x_ref[...]
    ms = jnp.mean(x * x, axis=-1, keepdims=True)
    y = (x * lax.rsqrt(ms + EPS) * w_ref[...]).astype(BF16)
    o_ref[...] = y
    ys = jnp.dot(p_ref[...], y, preferred_element_type=F32).astype(BF16)
    os_ref[...] = ys.reshape(os_ref.shape)


def _rmsnorm_slab(x, w, tm=512):
    t, d = x.shape
    tm = min(tm, t)
    cpt = tm // S5_CHUNK
    row = jnp.arange(tm)
    perm = (row[None, :] == ((row % cpt) * S5_CHUNK + row // cpt)[:, None]).astype(BF16)
    return pl.pallas_call(
        _rmsnorm_slab_kernel,
        grid=(t // tm,),
        in_specs=[pl.BlockSpec((tm, d), lambda i: (i, 0)),
                  pl.BlockSpec((1, d), lambda i: (0, 0)),
                  pl.BlockSpec((tm, tm), lambda i: (0, 0))],
        out_specs=[pl.BlockSpec((tm, d), lambda i: (i, 0)),
                   pl.BlockSpec((S5_CHUNK, cpt, d), lambda i: (0, i, 0))],
        out_shape=[jax.ShapeDtypeStruct((t, d), BF16),
                   jax.ShapeDtypeStruct((S5_CHUNK, t // S5_CHUNK, d), BF16)],
        compiler_params=_params(("parallel",)),
    )(x, w.reshape(1, d), perm)


def _inproj_kernel(a_ref, w_ref, o_ref, wb_ref):
    @pl.when(pl.program_id(1) == 0)
    def _():
        wb_ref[...] = w_ref[...].astype(BF16)

    o_ref[...] = jnp.dot(a_ref[...], wb_ref[...], preferred_element_type=F32).astype(o_ref.dtype)


def _inproj(h, w_in, layer, col_start, n_cols, tm=1024, tn=1024):
    t, k = h.shape
    tm, tn = min(tm, t), min(tn, n_cols)
    assert col_start % tn == 0
    c0 = col_start // tn
    return pl.pallas_call(
        _inproj_kernel,
        grid=(n_cols // tn, t // tm),
        in_specs=[pl.BlockSpec((tm, k), lambda j, i: (i, 0)),
                  _single((None, k, tn), lambda j, i: (layer, 0, c0 + j))],
        out_specs=pl.BlockSpec((tm, tn), lambda j, i: (i, j)),
        out_shape=jax.ShapeDtypeStruct((t, n_cols), BF16),
        scratch_shapes=[pltpu.VMEM((k, tn), BF16)],
        compiler_params=_params(("arbitrary", "arbitrary")),
    )(h, w_in)


def _inproj_t_kernel(h_ref, w_ref, o_ref, wt_ref):
    @pl.when(pl.program_id(1) == 0)
    def _():
        k, tc = w_ref.shape
        for k0 in range(0, k, tc):
            wt_ref[:, k0:k0 + tc] = w_ref[k0:k0 + tc, :].T.astype(BF16)

    acc = lax.dot_general(wt_ref[...], h_ref[...], (((1,), (1,)), ((), ())),
                          preferred_element_type=F32)
    o_ref[...] = acc.astype(BF16).reshape(o_ref.shape)


def _inproj_t(h_slab, w_in, layer, n_ch, tc=1024):
    n_slabs, n_rows, k = h_slab.shape
    tc = min(tc, n_ch)
    return pl.pallas_call(
        _inproj_t_kernel,
        grid=(n_ch // tc, n_slabs),
        in_specs=[pl.BlockSpec((None, n_rows, k), lambda c, j: (j, 0, 0)),
                  _single((None, k, tc), lambda c, j: (layer, 0, c))],
        out_specs=pl.BlockSpec((tc // SSM_GROUP, SSM_GROUP, n_rows), lambda c, j: (c, j, 0)),
        out_shape=jax.ShapeDtypeStruct((n_ch // SSM_GROUP, n_slabs * SSM_GROUP, n_rows), BF16),
        scratch_shapes=[pltpu.VMEM((tc, k), BF16)],
        compiler_params=_params(("arbitrary", "arbitrary")),
    )(h_slab, w_in)


def _outproj_kernel(ys_ref, yr_ref, ws_ref, wr_ref, x_ref, o_ref, wsb_ref, wrb_ref):
    @pl.when(pl.program_id(1) == 0)
    def _():
        wsb_ref[...] = ws_ref[...].astype(BF16)
        wrb_ref[...] = wr_ref[...].astype(BF16)

    acc = jnp.dot(ys_ref[...], wsb_ref[...], preferred_element_type=F32)
    acc = acc + jnp.dot(yr_ref[...], wrb_ref[...], preferred_element_type=F32)
    o_ref[...] = x_ref[...] + acc


def _outproj(y_ssm, y_ret, w_out, layer, x, tm=512, tn=1024):
    t, ds = y_ssm.shape
    _, dr = y_ret.shape
    assert ds == dr
    n = w_out.shape[-1]
    tm, tn = min(tm, t), min(tn, n)
    return pl.pallas_call(
        _outproj_kernel,
        grid=(n // tn, t // tm),
        in_specs=[pl.BlockSpec((tm, ds), lambda j, i: (i, 0)),
                  pl.BlockSpec((tm, dr), lambda j, i: (i, 0)),
                  _single((None, ds, tn), lambda j, i: (layer, 0, j)),
                  _single((None, dr, tn), lambda j, i: (layer, 1, j)),
                  pl.BlockSpec((tm, tn), lambda j, i: (i, j))],
        out_specs=pl.BlockSpec((tm, tn), lambda j, i: (i, j)),
        out_shape=jax.ShapeDtypeStruct((t, n), F32),
        scratch_shapes=[pltpu.VMEM((ds, tn), BF16), pltpu.VMEM((dr, tn), BF16)],
        compiler_params=_params(("arbitrary", "arbitrary")),
    )(y_ssm, y_ret, w_out, w_out, x)


def _s5_matrices(lam_re, lam_im, b_re, b_im, c_re, c_im, d_skip, log_dt, n_chunks):
    g, n = lam_re.shape
    h = SSM_GROUP
    c = S5_CHUNK
    hi = lax.Precision.HIGHEST
    dt = jnp.exp(log_dt)[:, None]
    tau = jnp.arange(c + 1, dtype=F32)[:, None, None]
    mag = jnp.exp(tau * (lam_re * dt)[None])
    ang = tau * (lam_im * dt)[None]
    p_re, p_im = mag * jnp.cos(ang), mag * jnp.sin(ang)
    ab_re, ab_im = p_re[1], p_im[1]
    num_re, num_im = ab_re - 1.0, ab_im
    den = lam_re * lam_re + lam_im * lam_im
    co_re = (num_re * lam_re + num_im * lam_im) / den
    co_im = (num_im * lam_re - num_re * lam_im) / den
    bb_re = co_re[..., None] * b_re - co_im[..., None] * b_im
    bb_im = co_re[..., None] * b_im + co_im[..., None] * b_re
    col = jnp.arange(c * h)
    spread_j = (col[None, :] // h == jnp.arange(c)[:, None]).astype(F32)
    spread_h = (col[None, :] % h == jnp.arange(h)[:, None]).astype(F32)
    q_re = jnp.einsum('jgn,jp->gnp', p_re[:c][::-1], spread_j, precision=hi)
    q_im = jnp.einsum('jgn,jp->gnp', p_im[:c][::-1], spread_j, precision=hi)
    x_re = jnp.einsum('gnk,kp->gnp', bb_re, spread_h, precision=hi)
    x_im = jnp.einsum('gnk,kp->gnp', bb_im, spread_h, precision=hi)
    e_t = jnp.concatenate([q_re * x_re - q_im * x_im, q_re * x_im + q_im * x_re], axis=1)
    c_a = jnp.concatenate([c_re, -c_im], axis=-1)
    c_b = jnp.concatenate([-c_im, -c_re], axis=-1)
    pt_re = p_re[1:].transpose(1, 0, 2)
    pt_im = p_im[1:].transpose(1, 0, 2)
    p_tab = jnp.concatenate([jnp.concatenate([pt_re, pt_re], axis=-1),
                             jnp.concatenate([pt_im, pt_im], axis=-1)], axis=1)
    n_steps = max(1, int(math.log2(n_chunks)))
    assert n_steps <= 8
    ar, ai = p_re[c], p_im[c]
    cols_re, cols_im = [], []
    for _ in range(n_steps):
        cols_re.append(ar)
        cols_im.append(ai)
        ar, ai = ar * ar - ai * ai, 2.0 * ar * ai
    pad = [jnp.zeros((g, n), F32)] * (8 - n_steps)
    stacked = jnp.stack(cols_re + pad + cols_im + pad, axis=0)
    a_pow = jnp.einsum('kgn,km->gnm', stacked, jnp.eye(16, 128, dtype=F32), precision=hi)
    d_col = jnp.tile(d_skip.reshape(g, 1, h), (1, c, 1)).reshape(g, c * h, 1)
    return e_t.astype(BF16), c_a, c_b, p_tab, a_pow, d_col


def _s5_group(u, e, c_a, c_b, p_tab, ap, d_col, n_chunks):
    gsz = c_a.shape[0]
    p = e.shape[1]
    n_pos = p // gsz
    k_rev = jnp.dot(c_a.astype(BF16), e, preferred_element_type=F32)
    lane = lax.broadcasted_iota(jnp.int32, k_rev.shape, 1)
    blocks = []
    for i in range(n_pos):
        shift = (n_pos - 1 - i) * gsz
        if shift == 0:
            blocks.append(k_rev)
        else:
            blocks.append(jnp.where(lane < p - shift, pltpu.roll(k_rev, p - shift, axis=1), 0.0))
    m_t = jnp.concatenate(blocks, axis=0).astype(BF16)
    f_t = jnp.concatenate([c_a * p_tab[i:i + 1] + c_b * p_tab[n_pos + i:n_pos + i + 1]
                           for i in range(n_pos)], axis=0).astype(BF16)

    y = jnp.dot(m_t, u, preferred_element_type=F32)
    s = jnp.dot(e, u, preferred_element_type=F32)
    n = s.shape[0] // 2
    cols = s.shape[1]
    chunk = lax.broadcasted_iota(jnp.int32, (n, cols), 1) % n_chunks
    z_re, z_im = s[:n], s[n:]
    step = 1
    m = 0
    while step < n_chunks:
        keep = chunk >= step
        s_re = jnp.where(keep, pltpu.roll(z_re, step, axis=1), 0.0)
        s_im = jnp.where(keep, pltpu.roll(z_im, step, axis=1), 0.0)
        a_re, a_im = ap[:, m:m + 1], ap[:, 8 + m:9 + m]
        z_re, z_im = z_re + a_re * s_re - a_im * s_im, z_im + a_re * s_im + a_im * s_re
        step *= 2
        m += 1
    first = chunk >= 1
    prev = jnp.concatenate([jnp.where(first, pltpu.roll(z_re, 1, axis=1), 0.0),
                            jnp.where(first, pltpu.roll(z_im, 1, axis=1), 0.0)], axis=0)
    y = y + jnp.dot(f_t, prev.astype(BF16), preferred_element_type=F32)
    return y + u.astype(F32) * d_col


def _s5_kernel(u_ref, e_ref, ca_ref, cb_ref, pt_ref, ap_ref, d_ref, y_ref, *, n_chunks):
    for gi in range(u_ref.shape[0]):
        y = _s5_group(u_ref[gi], e_ref[gi], ca_ref[gi], cb_ref[gi], pt_ref[gi], ap_ref[gi], d_ref[gi],
                      n_chunks)
        y_ref[gi] = y.astype(y_ref.dtype)


def _s5(u_t, mats, n_chunks, gb=4):
    g, p, r = u_t.shape
    e_t, c_a, c_b, p_tab, a_pow, d_col = mats
    n2 = e_t.shape[1]
    gb = min(gb, g)
    spec3 = lambda a, b: pl.BlockSpec((gb, a, b), lambda i: (i, 0, 0))
    return pl.pallas_call(
        functools.partial(_s5_kernel, n_chunks=n_chunks),
        grid=(g // gb,),
        in_specs=[spec3(p, r), spec3(n2, p), spec3(SSM_GROUP, n2), spec3(SSM_GROUP, n2),
                  spec3(2 * S5_CHUNK, n2), spec3(n2 // 2, 128), spec3(p, 1)],
        out_specs=spec3(p, r),
        out_shape=jax.ShapeDtypeStruct((g, p, r), BF16),
        compiler_params=_params(("parallel",)),
    )(u_t, e_t, c_a, c_b, p_tab, a_pow, d_col)


def _glu_kernel(yt_ref, w_ref, b_ref, nw_ref, g_ref, o_ref, wb_ref):
    @pl.when((pl.program_id(0) == 0) & (pl.program_id(1) == 0))
    def _():
        wb_ref[...] = w_ref[...].astype(BF16)

    groups, gsz, rows = yt_ref.shape
    y = yt_ref[...].reshape(groups * gsz, rows).astype(F32).T
    y = y * (0.5 * (1.0 + jnp.tanh(math.sqrt(2.0 / math.pi) * (y + 0.044715 * (y * y * y)))))
    z = jnp.dot(y.astype(BF16), wb_ref[...], preferred_element_type=F32) + b_ref[...]
    y = y * jax.nn.sigmoid(z)
    ms = jnp.mean(y * y, axis=-1, keepdims=True)
    y = y * lax.rsqrt(ms + EPS) * nw_ref[...]
    g = g_ref[...].astype(F32)
    o_ref[...] = (y * (g * jax.nn.sigmoid(g))).astype(o_ref.dtype)


def _glu(y_t, w_glu, layer, b_glu, norm_w, gate_slab, tr=256):
    groups, _, n_rows = y_t.shape
    d = groups * SSM_GROUP
    tr = min(tr, n_rows)
    out = pl.pallas_call(
        _glu_kernel,
        grid=(S5_CHUNK, n_rows // tr),
        in_specs=[pl.BlockSpec((groups, SSM_GROUP, tr), lambda i, r: (0, i, r)),
                  _single((None, d, d), lambda i, r: (layer, 0, 0)),
                  pl.BlockSpec((1, d), lambda i, r: (0, 0)),
                  pl.BlockSpec((1, d), lambda i, r: (0, 0)),
                  pl.BlockSpec((None, tr, d), lambda i, r: (i, r, 0))],
        out_specs=pl.BlockSpec((tr, d), lambda i, r: (r, i)),
        out_shape=jax.ShapeDtypeStruct((n_rows, S5_CHUNK * d), BF16),
        scratch_shapes=[pltpu.VMEM((d, d), BF16)],
        compiler_params=_params(("arbitrary", "arbitrary")),
    )(y_t, w_glu, b_glu.reshape(1, d), norm_w.reshape(1, d), gate_slab)
    return out.reshape(n_rows * S5_CHUNK, d)


def _retention_tables(seq, head_dim):
    pos = jnp.arange(seq, dtype=F32)
    inv_freq = ROPE_BASE ** (-jnp.arange(0, head_dim, 2, dtype=F32) / head_dim)
    ang = pos[:, None] * inv_freq[None, :]
    cos, sin = jnp.cos(ang), jnp.sin(ang)
    log_gamma = jnp.log1p(-jnp.power(2.0, -5.0 - jnp.arange(RET_HEADS, dtype=F32)))
    idx = jnp.arange(RET_CHUNK, dtype=F32)
    rel = idx[:, None] - idx[None, :]
    causal = rel >= 0
    decay_in = jnp.where(causal[None],
                         jnp.exp(jnp.where(causal, rel, 0.0)[None] * log_gamma[:, None, None]),
                         0.0)
    zeta = jnp.exp((RET_CHUNK - 1.0 - idx)[None, :] * log_gamma[:, None])
    xi = jnp.exp((idx + 1.0)[None, :] * log_gamma[:, None])
    gamma_c = jnp.exp(RET_CHUNK * log_gamma)
    bcast = lambda a: jnp.broadcast_to(a[:, :, None], a.shape + (head_dim,))
    gamma_row = jnp.broadcast_to(gamma_c[:, None, None], (RET_HEADS, 1, head_dim))
    return cos, sin, decay_in, bcast(zeta), bcast(xi), gamma_row


def _retention_kernel(q_ref, k_ref, v_ref, g_ref, cos_ref, sin_ref, dec_ref, zeta_ref, xi_ref,
                      gam_ref, nw_ref, o_ref, r_ref, *, n_chunks, head_dim):
    half = head_dim // 2
    scale = head_dim ** -0.5
    r_ref[...] = jnp.zeros_like(r_ref)

    def rope(t, cos, sin):
        t1, t2 = t[:, :half], t[:, half:]
        return jnp.concatenate([t1 * cos - t2 * sin, t2 * cos + t1 * sin], axis=-1)

    def body(c, carry):
        rows = pl.ds(pl.multiple_of(c * RET_CHUNK, RET_CHUNK), RET_CHUNK)
        cos, sin = cos_ref[rows, :], sin_ref[rows, :]
        q = rope(q_ref[rows, :].astype(F32), cos, sin)
        k = rope(k_ref[rows, :].astype(F32), cos, sin) * scale
        v = v_ref[rows, :]
        qb = q.astype(BF16)
        scores = lax.dot_general(qb, k.astype(BF16), (((1,), (1,)), ((), ())),
                                 preferred_element_type=F32) * dec_ref[0]
        out = jnp.dot(scores.astype(BF16), v, preferred_element_type=F32)
        out = out + jnp.dot(qb, r_ref[...].astype(BF16), preferred_element_type=F32) * xi_ref[0]
        kz = (k * zeta_ref[0]).T.astype(BF16)
        r_ref[...] = r_ref[...] * gam_ref[0] + jnp.dot(kz, v, preferred_element_type=F32)
        ms = jnp.mean(out * out, axis=-1, keepdims=True)
        g = g_ref[rows, :].astype(F32)
        out = out * lax.rsqrt(ms + EPS) * nw_ref[...] * (g * jax.nn.sigmoid(g))
        o_ref[rows, :] = out.astype(o_ref.dtype)
        return carry

    lax.fori_loop(0, n_chunks, body, 0, unroll=4)


def _retention(proj, ret_norm_w, tables, batch, seq, col0):
    t = proj.shape[0]
    d_ret = ret_norm_w.shape[0]
    hd = d_ret // RET_HEADS
    cos, sin, decay_in, zeta, xi, gamma_row = tables
    n_chunks = seq // RET_CHUNK

    def col(group):
        return pl.BlockSpec((seq, hd), lambda b, h: (b, col0 + group * RET_HEADS + h))

    per_head = lambda a, b_: pl.BlockSpec((1, a, b_), lambda b, h: (h, 0, 0))
    const2 = lambda a, b_: pl.BlockSpec((a, b_), lambda b, h: (0, 0))
    return pl.pallas_call(
        functools.partial(_retention_kernel, n_chunks=n_chunks, head_dim=hd),
        grid=(batch, RET_HEADS),
        in_specs=[col(0), col(1), col(2), col(3),
                  const2(seq, hd // 2), const2(seq, hd // 2),
                  per_head(RET_CHUNK, RET_CHUNK), per_head(RET_CHUNK, hd), per_head(RET_CHUNK, hd),
                  per_head(1, hd),
                  pl.BlockSpec((1, hd), lambda b, h: (0, h))],
        out_specs=pl.BlockSpec((seq, hd), lambda b, h: (b, h)),
        out_shape=jax.ShapeDtypeStruct((t, d_ret), BF16),
        scratch_shapes=[pltpu.VMEM((hd, hd), F32)],
        compiler_params=_params(("parallel", "parallel")),
    )(proj, proj, proj, proj, cos, sin, decay_in, zeta, xi, gamma_row, ret_norm_w.reshape(1, d_ret))


def _layer(x2, batch, seq, layer, norm_w, w_in, lam_re, lam_im, b_re, b_im, c_re, c_im, d_skip, log_dt,
           w_glu, b_glu, ssm_norm_w, ret_norm_w, w_out, ret_tables):
    t, d_model = x2.shape
    d_ssm = d_skip.shape[0]
    n_chunks = seq // S5_CHUNK
    n_rows = batch * n_chunks
    n_proj = w_in.shape[-1]

    h, h_slab = _rmsnorm_slab(x2, norm_w)
    proj = _inproj(h, w_in, layer, 2 * d_ssm, n_proj - 2 * d_ssm)
    gate_slab = _inproj(h_slab.reshape(t, d_model), w_in, layer, d_ssm, d_ssm)
    gate_slab = gate_slab.reshape(S5_CHUNK, n_rows, d_ssm)
    u_t = _inproj_t(h_slab, w_in, layer, d_ssm)
    mats = _s5_matrices(lam_re, lam_im, b_re, b_im, c_re, c_im, d_skip, log_dt, n_chunks)
    y_t = _s5(u_t, mats, n_chunks)
    y_ssm = _glu(y_t, w_glu, layer, b_glu, ssm_norm_w, gate_slab)
    y_ret = _retention(proj, ret_norm_w, ret_tables, batch, seq, 0)
    return _outproj(y_ssm, y_ret, w_out, layer, x2)


def kernel(x, norm_w, w_in, ssm_lambda_re, ssm_lambda_im, ssm_b_re, ssm_b_im, ssm_c_re, ssm_c_im,
           ssm_d, ssm_log_dt, ssm_w_glu, ssm_b_glu, ssm_norm_w, ret_norm_w, w_out, final_norm_w):
    batch, seq, d_model = x.shape
    depth = norm_w.shape[0]
    d_ret = ret_norm_w.shape[1]
    ret_tables = _retention_tables(seq, d_ret // RET_HEADS)
    h = x.reshape(batch * seq, d_model)
    for layer in range(depth):
        h = _layer(h, batch, seq, layer, norm_w[layer], w_in,
                   ssm_lambda_re[layer], ssm_lambda_im[layer], ssm_b_re[layer], ssm_b_im[layer],
                   ssm_c_re[layer], ssm_c_im[layer], ssm_d[layer], ssm_log_dt[layer],
                   ssm_w_glu, ssm_b_glu[layer], ssm_norm_w[layer], ret_norm_w[layer],
                   w_out, ret_tables)
    out = _rmsnorm(h, final_norm_w, F32)
    return out.reshape(batch, seq, d_model)
```

```python
import functools
import math

import jax
import jax.numpy as jnp
from jax import lax
from jax.experimental import pallas as pl
from jax.experimental.pallas import tpu as pltpu

F32 = jnp.float32
BF16 = jnp.bfloat16

EPS = 1e-6
SSM_GROUP = 16
RET_HEADS = 8
RET_CHUNK = 128
ROPE_BASE = 10000.0
S5_CHUNK = 16

VMEM_LIMIT = 56 * 1024 * 1024


def _params(sem):
    return pltpu.CompilerParams(dimension_semantics=sem, vmem_limit_bytes=VMEM_LIMIT)


def _sigmoid(x):
    return 0.5 * (1.0 + jnp.tanh(0.5 * x))


def _single(block_shape, index_map):
    return pl.BlockSpec(block_shape, index_map, pipeline_mode=pl.Buffered(1))


def _rmsnorm_kernel(x_ref, w_ref, o_ref):
    x = x_ref[...]
    ms = jnp.mean(x * x, axis=-1, keepdims=True)
    o_ref[...] = (x * lax.rsqrt(ms + EPS) * w_ref[...]).astype(o_ref.dtype)


def _rmsnorm(x, w, out_dtype, tm=512):
    t, d = x.shape
    tm = min(tm, t)
    return pl.pallas_call(
        _rmsnorm_kernel,
        grid=(t // tm,),
        in_specs=[pl.BlockSpec((tm, d), lambda i: (i, 0)),
                  pl.BlockSpec((1, d), lambda i: (0, 0))],
        out_specs=pl.BlockSpec((tm, d), lambda i: (i, 0)),
        out_shape=jax.ShapeDtypeStruct((t, d), out_dtype),
        compiler_params=_params(("parallel",)),
    )(x, w.reshape(1, d))


def _rmsnorm_slab_kernel(x_ref, w_ref, p_ref, o_ref, os_ref):
    x = x_ref[...]
    ms = jnp.mean(x * x, axis=-1, keepdims=True)
    y = (x * lax.rsqrt(ms + EPS) * w_ref[...]).astype(BF16)
    o_ref[...] = y
    ys = jnp.dot(p_ref[...], y, preferred_element_type=F32).astype(BF16)
    os_ref[...] = ys.reshape(os_ref.shape)


def _rmsnorm_slab(x, w, tm=512):
    t, d = x.shape
    tm = min(tm, t)
    cpt = tm // S5_CHUNK
    row = jnp.arange(tm)
    perm = (row[None, :] == ((row % cpt) * S5_CHUNK + row // cpt)[:, None]).astype(BF16)
    return pl.pallas_call(
        _rmsnorm_slab_kernel,
        grid=(t // tm,),
        in_specs=[pl.BlockSpec((tm, d), lambda i: (i, 0)),
                  pl.BlockSpec((1, d), lambda i: (0, 0)),
                  pl.BlockSpec((tm, tm), lambda i: (0, 0))],
        out_specs=[pl.BlockSpec((tm, d), lambda i: (i, 0)),
                   pl.BlockSpec((S5_CHUNK, cpt, d), lambda i: (0, i, 0))],
        out_shape=[jax.ShapeDtypeStruct((t, d), BF16),
                   jax.ShapeDtypeStruct((S5_CHUNK, t // S5_CHUNK, d), BF16)],
        compiler_params=_params(("parallel",)),
    )(x, w.reshape(1, d), perm)


def _inproj_kernel(a_ref, w_ref, o_ref, wb_ref):
    @pl.when(pl.program_id(1) == 0)
    def _():
        wb_ref[...] = w_ref[...].astype(BF16)

    o_ref[...] = jnp.dot(a_ref[...], wb_ref[...], preferred_element_type=F32).astype(o_ref.dtype)


def _inproj(h, w_in, layer, col_start, n_cols, tm=1024, tn=1024):
    t, k = h.shape
    tm, tn = min(tm, t), min(tn, n_cols)
    assert col_start % tn == 0
    c0 = col_start // tn
    return pl.pallas_call(
        _inproj_kernel,
        grid=(n_cols // tn, t // tm),
        in_specs=[pl.BlockSpec((tm, k), lambda j, i: (i, 0)),
                  _single((None, k, tn), lambda j, i: (layer, 0, c0 + j))],
        out_specs=pl.BlockSpec((tm, tn), lambda j, i: (i, j)),
        out_shape=jax.ShapeDtypeStruct((t, n_cols), BF16),
        scratch_shapes=[pltpu.VMEM((k, tn), BF16)],
        compiler_params=_params(("arbitrary", "arbitrary")),
    )(h, w_in)


def _inproj_t_kernel(h_ref, w_ref, o_ref, wt_ref):
    @pl.when(pl.program_id(1) == 0)
    def _():
        k, tc = w_ref.shape
        for k0 in range(0, k, tc):
            wt_ref[:, k0:k0 + tc] = w_ref[k0:k0 + tc, :].T.astype(BF16)

    acc = lax.dot_general(wt_ref[...], h_ref[...], (((1,), (1,)), ((), ())),
                          preferred_element_type=F32)
    o_ref[...] = acc.astype(BF16).reshape(o_ref.shape)


def _inproj_t(h_slab, w_in, layer, n_ch, tc=1024):
    n_slabs, n_rows, k = h_slab.shape
    tc = min(tc, n_ch)
    return pl.pallas_call(
        _inproj_t_kernel,
        grid=(n_ch // tc, n_slabs),
        in_specs=[pl.BlockSpec((None, n_rows, k), lambda c, j: (j, 0, 0)),
                  _single((None, k, tc), lambda c, j: (layer, 0, c))],
        out_specs=pl.BlockSpec((tc // SSM_GROUP, SSM_GROUP, n_rows), lambda c, j: (c, j, 0)),
        out_shape=jax.ShapeDtypeStruct((n_ch // SSM_GROUP, n_slabs * SSM_GROUP, n_rows), BF16),
        scratch_shapes=[pltpu.VMEM((tc, k), BF16)],
        compiler_params=_params(("arbitrary", "arbitrary")),
    )(h_slab, w_in)


def _outproj_kernel(ys_ref, yr_ref, ws_ref, wr_ref, x_ref, o_ref, wsb_ref, wrb_ref):
    @pl.when(pl.program_id(1) == 0)
    def _():
        wsb_ref[...] = ws_ref[...].astype(BF16)
        wrb_ref[...] = wr_ref[...].astype(BF16)

    acc = jnp.dot(ys_ref[...], wsb_ref[...], preferred_element_type=F32)
    acc = acc + jnp.dot(yr_ref[...], wrb_ref[...], preferred_element_type=F32)
    o_ref[...] = x_ref[...] + acc


def _outproj(y_ssm, y_ret, w_out, layer, x, tm=512, tn=1024):
    t, ds = y_ssm.shape
    _, dr = y_ret.shape
    assert ds == dr
    n = w_out.shape[-1]
    tm, tn = min(tm, t), min(tn, n)
    return pl.pallas_call(
        _outproj_kernel,
        grid=(n // tn, t // tm),
        in_specs=[pl.BlockSpec((tm, ds), lambda j, i: (i, 0)),
                  pl.BlockSpec((tm, dr), lambda j, i: (i, 0)),
                  _single((None, ds, tn), lambda j, i: (layer, 0, j)),
                  _single((None, dr, tn), lambda j, i: (layer, 1, j)),
                  pl.BlockSpec((tm, tn), lambda j, i: (i, j))],
        out_specs=pl.BlockSpec((tm, tn), lambda j, i: (i, j)),
        out_shape=jax.ShapeDtypeStruct((t, n), F32),
        scratch_shapes=[pltpu.VMEM((ds, tn), BF16), pltpu.VMEM((dr, tn), BF16)],
        compiler_params=_params(("arbitrary", "arbitrary")),
    )(y_ssm, y_ret, w_out, w_out, x)


def _s5_matrices(lam_re, lam_im, b_re, b_im, c_re, c_im, d_skip, log_dt, n_chunks):
    g, n = lam_re.shape
    h = SSM_GROUP
    c = S5_CHUNK
    hi = lax.Precision.HIGHEST
    dt = jnp.exp(log_dt)[:, None]
    tau = jnp.arange(c + 1, dtype=F32)[:, None, None]
    mag = jnp.exp(tau * (lam_re * dt)[None])
    ang = tau * (lam_im * dt)[None]
    p_re, p_im = mag * jnp.cos(ang), mag * jnp.sin(ang)
    ab_re, ab_im = p_re[1], p_im[1]
    num_re, num_im = ab_re - 1.0, ab_im
    den = lam_re * lam_re + lam_im * lam_im
    co_re = (num_re * lam_re + num_im * lam_im) / den
    co_im = (num_im * lam_re - num_re * lam_im) / den
    bb_re = co_re[..., None] * b_re - co_im[..., None] * b_im
    bb_im = co_re[..., None] * b_im + co_im[..., None] * b_re
    col = jnp.arange(c * h)
    spread_j = (col[None, :] // h == jnp.arange(c)[:, None]).astype(F32)
    spread_h = (col[None, :] % h == jnp.arange(h)[:, None]).astype(F32)
    q_re = jnp.einsum('jgn,jp->gnp', p_re[:c][::-1], spread_j, precision=hi)
    q_im = jnp.einsum('jgn,jp->gnp', p_im[:c][::-1], spread_j, precision=hi)
    x_re = jnp.einsum('gnk,kp->gnp', bb_re, spread_h, precision=hi)
    x_im = jnp.einsum('gnk,kp->gnp', bb_im, spread_h, precision=hi)
    e_t = jnp.concatenate([q_re * x_re - q_im * x_im, q_re * x_im + q_im * x_re], axis=1)
    c_a = jnp.concatenate([c_re, -c_im], axis=-1)
    c_b = jnp.concatenate([-c_im, -c_re], axis=-1)
    pt_re = p_re[1:].transpose(1, 0, 2)
    pt_im = p_im[1:].transpose(1, 0, 2)
    p_tab = jnp.concatenate([jnp.concatenate([pt_re, pt_re], axis=-1),
                             jnp.concatenate([pt_im, pt_im], axis=-1)], axis=1)
    n_steps = max(1, int(math.log2(n_chunks)))
    assert n_steps <= 8
    ar, ai = p_re[c], p_im[c]
    cols_re, cols_im = [], []
    for _ in range(n_steps):
        cols_re.append(ar)
        cols_im.append(ai)
        ar, ai = ar * ar - ai * ai, 2.0 * ar * ai
    pad = [jnp.zeros((g, n), F32)] * (8 - n_steps)
    stacked = jnp.stack(cols_re + pad + cols_im + pad, axis=0)
    a_pow = jnp.einsum('kgn,km->gnm', stacked, jnp.eye(16, 128, dtype=F32), precision=hi)
    d_col = jnp.tile(d_skip.reshape(g, 1, h), (1, c, 1)).reshape(g, c * h, 1)
    return e_t.astype(BF16), c_a, c_b, p_tab, a_pow, d_col


def _s5_group(u, e, c_a, c_b, p_tab, ap, d_col, n_chunks):
    gsz = c_a.shape[0]
    p = e.shape[1]
    n_pos = p // gsz
    k_rev = jnp.dot(c_a.astype(BF16), e, preferred_element_type=F32)
    lane = lax.broadcasted_iota(jnp.int32, k_rev.shape, 1)
    blocks = []
    for i in range(n_pos):
        shift = (n_pos - 1 - i) * gsz
        if shift == 0:
            blocks.append(k_rev)
        else:
            blocks.append(jnp.where(lane < p - shift, pltpu.roll(k_rev, p - shift, axis=1), 0.0))
    m_t = jnp.concatenate(blocks, axis=0).astype(BF16)
    f_t = jnp.concatenate([c_a * p_tab[i:i + 1] + c_b * p_tab[n_pos + i:n_pos + i + 1]
                           for i in range(n_pos)], axis=0).astype(BF16)

    y = jnp.dot(m_t, u, preferred_element_type=F32)
    s = jnp.dot(e, u, preferred_element_type=F32)
    n = s.shape[0] // 2
    cols = s.shape[1]
    chunk = lax.broadcasted_iota(jnp.int32, (n, cols), 1) % n_chunks
    z_re, z_im = s[:n], s[n:]
    step = 1
    m = 0
    while step < n_chunks:
        keep = chunk >= step
        s_re = jnp.where(keep, pltpu.roll(z_re, step, axis=1), 0.0)
        s_im = jnp.where(keep, pltpu.roll(z_im, step, axis=1), 0.0)
        a_re, a_im = ap[:, m:m + 1], ap[:, 8 + m:9 + m]
        z_re, z_im = z_re + a_re * s_re - a_im * s_im, z_im + a_re * s_im + a_im * s_re
        step *= 2
        m += 1
    first = chunk >= 1
    prev = jnp.concatenate([jnp.where(first, pltpu.roll(z_re, 1, axis=1), 0.0),
                            jnp.where(first, pltpu.roll(z_im, 1, axis=1), 0.0)], axis=0)
    y = y + jnp.dot(f_t, prev.astype(BF16), preferred_element_type=F32)
    return y + u.astype(F32) * d_col


def _s5_kernel(u_ref, e_ref, ca_ref, cb_ref, pt_ref, ap_ref, d_ref, y_ref, *, n_chunks):
    for gi in range(u_ref.shape[0]):
        y = _s5_group(u_ref[gi], e_ref[gi], ca_ref[gi], cb_ref[gi], pt_ref[gi], ap_ref[gi], d_ref[gi],
                      n_chunks)
        y_ref[gi] = y.astype(y_ref.dtype)


def _s5(u_t, mats, n_chunks, gb=4):
    g, p, r = u_t.shape
    e_t, c_a, c_b, p_tab, a_pow, d_col = mats
    n2 = e_t.shape[1]
    gb = min(gb, g)
    spec3 = lambda a, b: pl.BlockSpec((gb, a, b), lambda i: (i, 0, 0))
    return pl.pallas_call(
        functools.partial(_s5_kernel, n_chunks=n_chunks),
        grid=(g // gb,),
        in_specs=[spec3(p, r), spec3(n2, p), spec3(SSM_GROUP, n2), spec3(SSM_GROUP, n2),
                  spec3(2 * S5_CHUNK, n2), spec3(n2 // 2, 128), spec3(p, 1)],
        out_specs=spec3(p, r),
        out_shape=jax.ShapeDtypeStruct((g, p, r), BF16),
        compiler_params=_params(("parallel",)),
    )(u_t, e_t, c_a, c_b, p_tab, a_pow, d_col)


def _glu_kernel(yt_ref, w_ref, b_ref, nw_ref, g_ref, o_ref, wb_ref):
    @pl.when((pl.program_id(0) == 0) & (pl.program_id(1) == 0))
    def _():
        wb_ref[...] = w_ref[...].astype(BF16)

    groups, gsz, rows = yt_ref.shape
    y = yt_ref[...].reshape(groups * gsz, rows).astype(F32).T
    y = y * (0.5 * (1.0 + jnp.tanh(math.sqrt(2.0 / math.pi) * (y + 0.044715 * (y * y * y)))))
    z = jnp.dot(y.astype(BF16), wb_ref[...], preferred_element_type=F32) + b_ref[...]
    y = y * _sigmoid(z)
    ms = jnp.mean(y * y, axis=-1, keepdims=True)
    y = y * lax.rsqrt(ms + EPS) * nw_ref[...]
    g = g_ref[...].astype(F32)
    o_ref[...] = (y * (g * _sigmoid(g))).astype(o_ref.dtype)


def _glu(y_t, w_glu, layer, b_glu, norm_w, gate_slab, tr=256):
    groups, _, n_rows = y_t.shape
    d = groups * SSM_GROUP
    tr = min(tr, n_rows)
    out = pl.pallas_call(
        _glu_kernel,
        grid=(S5_CHUNK, n_rows // tr),
        in_specs=[pl.BlockSpec((groups, SSM_GROUP, tr), lambda i, r: (0, i, r)),
                  _single((None, d, d), lambda i, r: (layer, 0, 0)),
                  pl.BlockSpec((1, d), lambda i, r: (0, 0)),
                  pl.BlockSpec((1, d), lambda i, r: (0, 0)),
                  pl.BlockSpec((None, tr, d), lambda i, r: (i, r, 0))],
        out_specs=pl.BlockSpec((None, tr, d), lambda i, r: (i, r, 0)),
        out_shape=jax.ShapeDtypeStruct((S5_CHUNK, n_rows, d), BF16),
        scratch_shapes=[pltpu.VMEM((d, d), BF16)],
        compiler_params=_params(("arbitrary", "arbitrary")),
    )(y_t, w_glu, b_glu.reshape(1, d), norm_w.reshape(1, d), gate_slab)
    return _unslab(out)


def _unslab_kernel(p_ref, y_ref, o_ref):
    n_slabs, cpt, d = y_ref.shape
    y = y_ref[...].reshape(n_slabs * cpt, d)
    o_ref[...] = jnp.dot(p_ref[...], y, preferred_element_type=F32).astype(o_ref.dtype)


def _unslab(y_slab, tm=512):
    n_slabs, n_rows, d = y_slab.shape
    t = n_slabs * n_rows
    tm = min(tm, t)
    cpt = tm // n_slabs
    row = jnp.arange(tm)
    perm_t = (row[None, :] == ((row % n_slabs) * cpt + row // n_slabs)[:, None]).astype(BF16)
    return pl.pallas_call(
        _unslab_kernel,
        grid=(t // tm,),
        in_specs=[pl.BlockSpec((tm, tm), lambda i: (0, 0)),
                  pl.BlockSpec((n_slabs, cpt, d), lambda i: (0, i, 0))],
        out_specs=pl.BlockSpec((tm, d), lambda i: (i, 0)),
        out_shape=jax.ShapeDtypeStruct((t, d), y_slab.dtype),
        compiler_params=_params(("parallel",)),
    )(perm_t, y_slab)


def _retention_tables(seq, head_dim):
    pos = jnp.arange(seq, dtype=F32)
    inv_freq = ROPE_BASE ** (-jnp.arange(0, head_dim, 2, dtype=F32) / head_dim)
    ang = pos[:, None] * inv_freq[None, :]
    cos, sin = jnp.cos(ang), jnp.sin(ang)
    log_gamma = jnp.log1p(-jnp.power(2.0, -5.0 - jnp.arange(RET_HEADS, dtype=F32)))
    idx = jnp.arange(RET_CHUNK, dtype=F32)
    rel = idx[:, None] - idx[None, :]
    causal = rel >= 0
    decay_in = jnp.where(causal[None],
                         jnp.exp(jnp.where(causal, rel, 0.0)[None] * log_gamma[:, None, None]),
                         0.0)
    zeta = jnp.exp((RET_CHUNK - 1.0 - idx)[None, :] * log_gamma[:, None])
    xi = jnp.exp((idx + 1.0)[None, :] * log_gamma[:, None])
    gamma_c = jnp.exp(RET_CHUNK * log_gamma)
    bcast = lambda a: jnp.broadcast_to(a[:, :, None], a.shape + (head_dim,))
    gamma_row = jnp.broadcast_to(gamma_c[:, None, None], (RET_HEADS, 1, head_dim))
    return cos, sin, decay_in, bcast(zeta), bcast(xi), gamma_row


def _retention_kernel(q_ref, k_ref, v_ref, g_ref, cos_ref, sin_ref, dec_ref, zeta_ref, xi_ref,
                      gam_ref, nw_ref, o_ref, r_ref, *, n_chunks, head_dim):
    half = head_dim // 2
    scale = head_dim ** -0.5
    r_ref[...] = jnp.zeros_like(r_ref)

    def rope(t, cos, sin):
        t1, t2 = t[:, :half], t[:, half:]
        return jnp.concatenate([t1 * cos - t2 * sin, t2 * cos + t1 * sin], axis=-1)

    def body(c, carry):
        rows = pl.ds(pl.multiple_of(c * RET_CHUNK, RET_CHUNK), RET_CHUNK)
        cos, sin = cos_ref[rows, :], sin_ref[rows, :]
        q = rope(q_ref[rows, :].astype(F32), cos, sin)
        k = rope(k_ref[rows, :].astype(F32), cos, sin) * scale
        v = v_ref[rows, :]
        qb = q.astype(BF16)
        scores = lax.dot_general(qb, k.astype(BF16), (((1,), (1,)), ((), ())),
                                 preferred_element_type=F32) * dec_ref[0]
        out = jnp.dot(scores.astype(BF16), v, preferred_element_type=F32)
        out = out + jnp.dot(qb, r_ref[...].astype(BF16), preferred_element_type=F32) * xi_ref[0]
        kz = (k * zeta_ref[0]).T.astype(BF16)
        r_ref[...] = r_ref[...] * gam_ref[0] + jnp.dot(kz, v, preferred_element_type=F32)
        ms = jnp.mean(out * out, axis=-1, keepdims=True)
        g = g_ref[rows, :].astype(F32)
        out = out * lax.rsqrt(ms + EPS) * nw_ref[...] * (g * _sigmoid(g))
        o_ref[rows, :] = out.astype(o_ref.dtype)
        return carry

    lax.fori_loop(0, n_chunks, body, 0, unroll=8)


def _retention(proj, ret_norm_w, tables, batch, seq, col0):
    t = proj.shape[0]
    d_ret = ret_norm_w.shape[0]
    hd = d_ret // RET_HEADS
    cos, sin, decay_in, zeta, xi, gamma_row = tables
    n_chunks = seq // RET_CHUNK

    def col(group):
        return pl.BlockSpec((seq, hd), lambda b, h: (b, col0 + group * RET_HEADS + h))

    per_head = lambda a, b_: pl.BlockSpec((1, a, b_), lambda b, h: (h, 0, 0))
    const2 = lambda a, b_: pl.BlockSpec((a, b_), lambda b, h: (0, 0))
    return pl.pallas_call(
        functools.partial(_retention_kernel, n_chunks=n_chunks, head_dim=hd),
        grid=(batch, RET_HEADS),
        in_specs=[col(0), col(1), col(2), col(3),
                  const2(seq, hd // 2), const2(seq, hd // 2),
                  per_head(RET_CHUNK, RET_CHUNK), per_head(RET_CHUNK, hd), per_head(RET_CHUNK, hd),
                  per_head(1, hd),
                  pl.BlockSpec((1, hd), lambda b, h: (0, h))],
        out_specs=pl.BlockSpec((seq, hd), lambda b, h: (b, h)),
        out_shape=jax.ShapeDtypeStruct((t, d_ret), BF16),
        scratch_shapes=[pltpu.VMEM((hd, hd), F32)],
        compiler_params=_params(("parallel", "parallel")),
    )(proj, proj, proj, proj, cos, sin, decay_in, zeta, xi, gamma_row, ret_norm_w.reshape(1, d_ret))


def _layer(x2, batch, seq, layer, norm_w, w_in, lam_re, lam_im, b_re, b_im, c_re, c_im, d_skip, log_dt,
           w_glu, b_glu, ssm_norm_w, ret_norm_w, w_out, ret_tables):
    t, d_model = x2.shape
    d_ssm = d_skip.shape[0]
    n_chunks = seq // S5_CHUNK
    n_rows = batch * n_chunks
    n_proj = w_in.shape[-1]

    h, h_slab = _rmsnorm_slab(x2, norm_w)
    proj = _inproj(h, w_in, layer, 2 * d_ssm, n_proj - 2 * d_ssm)
    gate_slab = _inproj(h_slab.reshape(t, d_model), w_in, layer, d_ssm, d_ssm)
    gate_slab = gate_slab.reshape(S5_CHUNK, n_rows, d_ssm)
    u_t = _inproj_t(h_slab, w_in, layer, d_ssm)
    mats = _s5_matrices(lam_re, lam_im, b_re, b_im, c_re, c_im, d_skip, log_dt, n_chunks)
    y_t = _s5(u_t, mats, n_chunks)
    y_ssm = _glu(y_t, w_glu, layer, b_glu, ssm_norm_w, gate_slab)
    y_ret = _retention(proj, ret_norm_w, ret_tables, batch, seq, 0)
    return _outproj(y_ssm, y_ret, w_out, layer, x2)


def kernel(x, norm_w, w_in, ssm_lambda_re, ssm_lambda_im, ssm_b_re, ssm_b_im, ssm_c_re, ssm_c_im,
           ssm_d, ssm_log_dt, ssm_w_glu, ssm_b_glu, ssm_norm_w, ret_norm_w, w_out, final_norm_w):
    batch, seq, d_model = x.shape
    depth = norm_w.shape[0]
    d_ret = ret_norm_w.shape[1]
    ret_tables = _retention_tables(seq, d_ret // RET_HEADS)
    h = x.reshape(batch * seq, d_model)
    for layer in range(depth):
        h = _layer(h, batch, seq, layer, norm_w[layer], w_in,
                   ssm_lambda_re[layer], ssm_lambda_im[layer], ssm_b_re[layer], ssm_b_im[layer],
                   ssm_c_re[layer], ssm_c_im[layer], ssm_d[layer], ssm_log_dt[layer],
                   ssm_w_glu, ssm_b_glu[layer], ssm_norm_w[layer], ret_norm_w[layer],
                   w_out, ret_tables)
    out = _rmsnorm(h, final_norm_w, F32)
    return out.reshape(batch, seq, d_model)
```

```python
import functools
import math

import jax
import jax.numpy as jnp
from jax import lax
from jax.experimental import pallas as pl
from jax.experimental.pallas import tpu as pltpu

F32 = jnp.float32
BF16 = jnp.bfloat16

EPS = 1e-6
SSM_GROUP = 16
RET_HEADS = 8
RET_CHUNK = 128
ROPE_BASE = 10000.0
S5_CHUNK = 16

VMEM_LIMIT = 56 * 1024 * 1024


def _params(sem):
    return pltpu.CompilerParams(dimension_semantics=sem, vmem_limit_bytes=VMEM_LIMIT)


def _sigmoid(x):
    return 0.5 * (1.0 + jnp.tanh(0.5 * x))


def _single(block_shape, index_map):
    return pl.BlockSpec(block_shape, index_map, pipeline_mode=pl.Buffered(1))


def _rmsnorm_kernel(x_ref, w_ref, o_ref):
    x = x_ref[...]
    ms = jnp.mean(x * x, axis=-1, keepdims=True)
    o_ref[...] = (x * lax.rsqrt(ms + EPS) * w_ref[...]).astype(o_ref.dtype)


def _rmsnorm(x, w, out_dtype, tm=512):
    t, d = x.shape
    tm = min(tm, t)
    return pl.pallas_call(
        _rmsnorm_kernel,
        grid=(t // tm,),
        in_specs=[pl.BlockSpec((tm, d), lambda i: (i, 0)),
                  pl.BlockSpec((1, d), lambda i: (0, 0))],
        out_specs=pl.BlockSpec((tm, d), lambda i: (i, 0)),
        out_shape=jax.ShapeDtypeStruct((t, d), out_dtype),
        compiler_params=_params(("parallel",)),
    )(x, w.reshape(1, d))


def _rmsnorm_slab_kernel(x_ref, w_ref, p_ref, o_ref, os_ref):
    x = x_ref[...]
    ms = jnp.mean(x * x, axis=-1, keepdims=True)
    y = (x * lax.rsqrt(ms + EPS) * w_ref[...]).astype(BF16)
    o_ref[...] = y
    ys = jnp.dot(p_ref[...], y, preferred_element_type=F32).astype(BF16)
    os_ref[...] = ys.reshape(os_ref.shape)


def _rmsnorm_slab(x, w, tm=512):
    t, d = x.shape
    tm = min(tm, t)
    cpt = tm // S5_CHUNK
    row = jnp.arange(tm)
    perm = (row[None, :] == ((row % cpt) * S5_CHUNK + row // cpt)[:, None]).astype(BF16)
    return pl.pallas_call(
        _rmsnorm_slab_kernel,
        grid=(t // tm,),
        in_specs=[pl.BlockSpec((tm, d), lambda i: (i, 0)),
                  pl.BlockSpec((1, d), lambda i: (0, 0)),
                  pl.BlockSpec((tm, tm), lambda i: (0, 0))],
        out_specs=[pl.BlockSpec((tm, d), lambda i: (i, 0)),
                   pl.BlockSpec((S5_CHUNK, cpt, d), lambda i: (0, i, 0))],
        out_shape=[jax.ShapeDtypeStruct((t, d), BF16),
                   jax.ShapeDtypeStruct((S5_CHUNK, t // S5_CHUNK, d), BF16)],
        compiler_params=_params(("parallel",)),
    )(x, w.reshape(1, d), perm)


def _inproj_kernel(a_ref, w_ref, o_ref, wb_ref):
    @pl.when(pl.program_id(1) == 0)
    def _():
        wb_ref[...] = w_ref[...].astype(BF16)

    o_ref[...] = jnp.dot(a_ref[...], wb_ref[...], preferred_element_type=F32).astype(o_ref.dtype)


def _inproj(h, w_in, layer, col_start, n_cols, tm=1024, tn=1024):
    t, k = h.shape
    tm, tn = min(tm, t), min(tn, n_cols)
    assert col_start % tn == 0
    c0 = col_start // tn
    return pl.pallas_call(
        _inproj_kernel,
        grid=(n_cols // tn, t // tm),
        in_specs=[pl.BlockSpec((tm, k), lambda j, i: (i, 0)),
                  _single((None, k, tn), lambda j, i: (layer, 0, c0 + j))],
        out_specs=pl.BlockSpec((tm, tn), lambda j, i: (i, j)),
        out_shape=jax.ShapeDtypeStruct((t, n_cols), BF16),
        scratch_shapes=[pltpu.VMEM((k, tn), BF16)],
        compiler_params=_params(("arbitrary", "arbitrary")),
    )(h, w_in)


def _inproj_t_kernel(h_ref, w_ref, o_ref, wt_ref):
    @pl.when(pl.program_id(1) == 0)
    def _():
        k, tc = w_ref.shape
        for k0 in range(0, k, tc):
            wt_ref[:, k0:k0 + tc] = w_ref[k0:k0 + tc, :].T.astype(BF16)

    acc = lax.dot_general(wt_ref[...], h_ref[...], (((1,), (1,)), ((), ())),
                          preferred_element_type=F32)
    o_ref[...] = acc.astype(BF16).reshape(o_ref.shape)


def _inproj_t(h_slab, w_in, layer, n_ch, tc=1024):
    n_slabs, n_rows, k = h_slab.shape
    tc = min(tc, n_ch)
    return pl.pallas_call(
        _inproj_t_kernel,
        grid=(n_ch // tc, n_slabs),
        in_specs=[pl.BlockSpec((None, n_rows, k), lambda c, j: (j, 0, 0)),
                  _single((None, k, tc), lambda c, j: (layer, 0, c))],
        out_specs=pl.BlockSpec((tc // SSM_GROUP, SSM_GROUP, n_rows), lambda c, j: (c, j, 0)),
        out_shape=jax.ShapeDtypeStruct((n_ch // SSM_GROUP, n_slabs * SSM_GROUP, n_rows), BF16),
        scratch_shapes=[pltpu.VMEM((tc, k), BF16)],
        compiler_params=_params(("arbitrary", "arbitrary")),
    )(h_slab, w_in)


def _outproj_kernel(ys_ref, yr_ref, ws_ref, wr_ref, x_ref, o_ref, wsb_ref, wrb_ref):
    @pl.when(pl.program_id(1) == 0)
    def _():
        wsb_ref[...] = ws_ref[...].astype(BF16)
        wrb_ref[...] = wr_ref[...].astype(BF16)

    acc = jnp.dot(ys_ref[...], wsb_ref[...], preferred_element_type=F32)
    acc = acc + jnp.dot(yr_ref[...], wrb_ref[...], preferred_element_type=F32)
    o_ref[...] = x_ref[...] + acc


def _outproj(y_ssm, y_ret, w_out, layer, x, tm=512, tn=1024):
    t, ds = y_ssm.shape
    _, dr = y_ret.shape
    assert ds == dr
    n = w_out.shape[-1]
    tm, tn = min(tm, t), min(tn, n)
    return pl.pallas_call(
        _outproj_kernel,
        grid=(n // tn, t // tm),
        in_specs=[pl.BlockSpec((tm, ds), lambda j, i: (i, 0)),
                  pl.BlockSpec((tm, dr), lambda j, i: (i, 0)),
                  _single((None, ds, tn), lambda j, i: (layer, 0, j)),
                  _single((None, dr, tn), lambda j, i: (layer, 1, j)),
                  pl.BlockSpec((tm, tn), lambda j, i: (i, j))],
        out_specs=pl.BlockSpec((tm, tn), lambda j, i: (i, j)),
        out_shape=jax.ShapeDtypeStruct((t, n), F32),
        scratch_shapes=[pltpu.VMEM((ds, tn), BF16), pltpu.VMEM((dr, tn), BF16)],
        compiler_params=_params(("arbitrary", "arbitrary")),
    )(y_ssm, y_ret, w_out, w_out, x)


def _s5_matrices(lam_re, lam_im, b_re, b_im, c_re, c_im, d_skip, log_dt, n_chunks):
    g, n = lam_re.shape
    h = SSM_GROUP
    c = S5_CHUNK
    hi = lax.Precision.HIGHEST
    dt = jnp.exp(log_dt)[:, None]
    tau = jnp.arange(c + 1, dtype=F32)[:, None, None]
    mag = jnp.exp(tau * (lam_re * dt)[None])
    ang = tau * (lam_im * dt)[None]
    p_re, p_im = mag * jnp.cos(ang), mag * jnp.sin(ang)
    ab_re, ab_im = p_re[1], p_im[1]
    num_re, num_im = ab_re - 1.0, ab_im
    den = lam_re * lam_re + lam_im * lam_im
    co_re = (num_re * lam_re + num_im * lam_im) / den
    co_im = (num_im * lam_re - num_re * lam_im) / den
    bb_re = co_re[..., None] * b_re - co_im[..., None] * b_im
    bb_im = co_re[..., None] * b_im + co_im[..., None] * b_re
    col = jnp.arange(c * h)
    spread_j = (col[None, :] // h == jnp.arange(c)[:, None]).astype(F32)
    spread_h = (col[None, :] % h == jnp.arange(h)[:, None]).astype(F32)
    q_re = jnp.einsum('jgn,jp->gnp', p_re[:c][::-1], spread_j, precision=hi)
    q_im = jnp.einsum('jgn,jp->gnp', p_im[:c][::-1], spread_j, precision=hi)
    x_re = jnp.einsum('gnk,kp->gnp', bb_re, spread_h, precision=hi)
    x_im = jnp.einsum('gnk,kp->gnp', bb_im, spread_h, precision=hi)
    e_t = jnp.concatenate([q_re * x_re - q_im * x_im, q_re * x_im + q_im * x_re], axis=1)
    c_a = jnp.concatenate([c_re, -c_im], axis=-1)
    c_b = jnp.concatenate([-c_im, -c_re], axis=-1)
    pt_re = p_re[1:].transpose(1, 0, 2)
    pt_im = p_im[1:].transpose(1, 0, 2)
    p_tab = jnp.concatenate([jnp.concatenate([pt_re, pt_re], axis=-1),
                             jnp.concatenate([pt_im, pt_im], axis=-1)], axis=1)
    n_steps = max(1, int(math.log2(n_chunks)))
    assert n_steps <= 8
    ar, ai = p_re[c], p_im[c]
    cols_re, cols_im = [], []
    for _ in range(n_steps):
        cols_re.append(ar)
        cols_im.append(ai)
        ar, ai = ar * ar - ai * ai, 2.0 * ar * ai
    pad = [jnp.zeros((g, n), F32)] * (8 - n_steps)
    stacked = jnp.stack(cols_re + pad + cols_im + pad, axis=0)
    a_pow = jnp.einsum('kgn,km->gnm', stacked, jnp.eye(16, 128, dtype=F32), precision=hi)
    d_col = jnp.tile(d_skip.reshape(g, 1, h), (1, c, 1)).reshape(g, c * h, 1)
    return e_t.astype(BF16), c_a, c_b, p_tab, a_pow, d_col


def _s5_group(u, e, c_a, c_b, p_tab, ap, d_col, n_chunks):
    gsz = c_a.shape[0]
    p = e.shape[1]
    n_pos = p // gsz
    k_rev = jnp.dot(c_a.astype(BF16), e, preferred_element_type=F32)
    lane = lax.broadcasted_iota(jnp.int32, k_rev.shape, 1)
    blocks = []
    for i in range(n_pos):
        shift = (n_pos - 1 - i) * gsz
        if shift == 0:
            blocks.append(k_rev)
        else:
            blocks.append(jnp.where(lane < p - shift, pltpu.roll(k_rev, p - shift, axis=1), 0.0))
    m_t = jnp.concatenate(blocks, axis=0).astype(BF16)
    f_t = jnp.concatenate([c_a * p_tab[i:i + 1] + c_b * p_tab[n_pos + i:n_pos + i + 1]
                           for i in range(n_pos)], axis=0).astype(BF16)

    y = jnp.dot(m_t, u, preferred_element_type=F32)
    s = jnp.dot(e, u, preferred_element_type=F32)
    n = s.shape[0] // 2
    cols = s.shape[1]
    chunk = lax.broadcasted_iota(jnp.int32, (n, cols), 1) % n_chunks
    z_re, z_im = s[:n], s[n:]
    step = 1
    m = 0
    while step < n_chunks:
        keep = chunk >= step
        s_re = jnp.where(keep, pltpu.roll(z_re, step, axis=1), 0.0)
        s_im = jnp.where(keep, pltpu.roll(z_im, step, axis=1), 0.0)
        a_re, a_im = ap[:, m:m + 1], ap[:, 8 + m:9 + m]
        z_re, z_im = z_re + a_re * s_re - a_im * s_im, z_im + a_re * s_im + a_im * s_re
        step *= 2
        m += 1
    first = chunk >= 1
    prev = jnp.concatenate([jnp.where(first, pltpu.roll(z_re, 1, axis=1), 0.0),
                            jnp.where(first, pltpu.roll(z_im, 1, axis=1), 0.0)], axis=0)
    y = y + jnp.dot(f_t, prev.astype(BF16), preferred_element_type=F32)
    return y + u.astype(F32) * d_col


def _s5_kernel(u_ref, e_ref, ca_ref, cb_ref, pt_ref, ap_ref, d_ref, y_ref, *, n_chunks):
    for gi in range(u_ref.shape[0]):
        y = _s5_group(u_ref[gi], e_ref[gi], ca_ref[gi], cb_ref[gi], pt_ref[gi], ap_ref[gi], d_ref[gi],
                      n_chunks)
        y_ref[gi] = y.astype(y_ref.dtype)


def _s5(u_t, mats, layer, n_chunks, gb=8):
    g, p, r = u_t.shape
    e_t, c_a, c_b, p_tab, a_pow, d_col = mats
    n2 = e_t.shape[1]
    gb = min(gb, g)
    steps = g // gb
    spec3 = lambda a, b: pl.BlockSpec((gb, a, b), lambda i: (i, 0, 0))
    tab3 = lambda a, b: pl.BlockSpec((gb, a, b), lambda i: (layer * steps + i, 0, 0))
    return pl.pallas_call(
        functools.partial(_s5_kernel, n_chunks=n_chunks),
        grid=(steps,),
        in_specs=[spec3(p, r), tab3(n2, p), tab3(SSM_GROUP, n2), tab3(SSM_GROUP, n2),
                  tab3(2 * S5_CHUNK, n2), tab3(n2 // 2, 128), tab3(p, 1)],
        out_specs=spec3(p, r),
        out_shape=jax.ShapeDtypeStruct((g, p, r), BF16),
        compiler_params=_params(("parallel",)),
    )(u_t, e_t, c_a, c_b, p_tab, a_pow, d_col)


def _glu_kernel(yt_ref, w_ref, b_ref, nw_ref, g_ref, o_ref, wb_ref):
    @pl.when((pl.program_id(0) == 0) & (pl.program_id(1) == 0))
    def _():
        wb_ref[...] = w_ref[...].astype(BF16)

    groups, gsz, rows = yt_ref.shape
    y = yt_ref[...].reshape(groups * gsz, rows).astype(F32).T
    y = y * (0.5 * (1.0 + jnp.tanh(math.sqrt(2.0 / math.pi) * (y + 0.044715 * (y * y * y)))))
    z = jnp.dot(y.astype(BF16), wb_ref[...], preferred_element_type=F32) + b_ref[...]
    y = y * _sigmoid(z)
    ms = jnp.mean(y * y, axis=-1, keepdims=True)
    y = y * lax.rsqrt(ms + EPS) * nw_ref[...]
    g = g_ref[...].astype(F32)
    o_ref[...] = (y * (g * _sigmoid(g))).astype(o_ref.dtype)


def _glu(y_t, w_glu, layer, b_glu, norm_w, gate_slab, tr=256):
    groups, _, n_rows = y_t.shape
    d = groups * SSM_GROUP
    tr = min(tr, n_rows)
    out = pl.pallas_call(
        _glu_kernel,
        grid=(S5_CHUNK, n_rows // tr),
        in_specs=[pl.BlockSpec((groups, SSM_GROUP, tr), lambda i, r: (0, i, r)),
                  _single((None, d, d), lambda i, r: (layer, 0, 0)),
                  pl.BlockSpec((1, d), lambda i, r: (0, 0)),
                  pl.BlockSpec((1, d), lambda i, r: (0, 0)),
                  pl.BlockSpec((None, tr, d), lambda i, r: (i, r, 0))],
        out_specs=pl.BlockSpec((None, tr, d), lambda i, r: (i, r, 0)),
        out_shape=jax.ShapeDtypeStruct((S5_CHUNK, n_rows, d), BF16),
        scratch_shapes=[pltpu.VMEM((d, d), BF16)],
        compiler_params=_params(("arbitrary", "arbitrary")),
    )(y_t, w_glu, b_glu.reshape(1, d), norm_w.reshape(1, d), gate_slab)
    return _unslab(out)


def _unslab_kernel(p_ref, y_ref, o_ref):
    n_slabs, cpt, d = y_ref.shape
    y = y_ref[...].reshape(n_slabs * cpt, d)
    o_ref[...] = jnp.dot(p_ref[...], y, preferred_element_type=F32).astype(o_ref.dtype)


def _unslab(y_slab, tm=512):
    n_slabs, n_rows, d = y_slab.shape
    t = n_slabs * n_rows
    tm = min(tm, t)
    cpt = tm // n_slabs
    row = jnp.arange(tm)
    perm_t = (row[None, :] == ((row % n_slabs) * cpt + row // n_slabs)[:, None]).astype(BF16)
    return pl.pallas_call(
        _unslab_kernel,
        grid=(t // tm,),
        in_specs=[pl.BlockSpec((tm, tm), lambda i: (0, 0)),
                  pl.BlockSpec((n_slabs, cpt, d), lambda i: (0, i, 0))],
        out_specs=pl.BlockSpec((tm, d), lambda i: (i, 0)),
        out_shape=jax.ShapeDtypeStruct((t, d), y_slab.dtype),
        compiler_params=_params(("parallel",)),
    )(perm_t, y_slab)


def _retention_tables(seq, head_dim):
    pos = jnp.arange(seq, dtype=F32)
    inv_freq = ROPE_BASE ** (-jnp.arange(0, head_dim, 2, dtype=F32) / head_dim)
    ang = pos[:, None] * inv_freq[None, :]
    cos, sin = jnp.cos(ang), jnp.sin(ang)
    log_gamma = jnp.log1p(-jnp.power(2.0, -5.0 - jnp.arange(RET_HEADS, dtype=F32)))
    idx = jnp.arange(RET_CHUNK, dtype=F32)
    rel = idx[:, None] - idx[None, :]
    causal = rel >= 0
    decay_in = jnp.where(causal[None],
                         jnp.exp(jnp.where(causal, rel, 0.0)[None] * log_gamma[:, None, None]),
                         0.0)
    zeta = jnp.exp((RET_CHUNK - 1.0 - idx)[None, :] * log_gamma[:, None])
    xi = jnp.exp((idx + 1.0)[None, :] * log_gamma[:, None])
    gamma_c = jnp.exp(RET_CHUNK * log_gamma)
    bcast = lambda a: jnp.broadcast_to(a[:, :, None], a.shape + (head_dim,))
    gamma_row = jnp.broadcast_to(gamma_c[:, None, None], (RET_HEADS, 1, head_dim))
    return cos, sin, decay_in, bcast(zeta), bcast(xi), gamma_row


def _retention_kernel(q_ref, k_ref, v_ref, g_ref, cos_ref, sin_ref, dec_ref, zeta_ref, xi_ref,
                      gam_ref, nw_ref, o_ref, r_ref, *, n_chunks, head_dim):
    half = head_dim // 2
    scale = head_dim ** -0.5
    r_ref[...] = jnp.zeros_like(r_ref)

    def rope(t, cos, sin):
        t1, t2 = t[:, :half], t[:, half:]
        return jnp.concatenate([t1 * cos - t2 * sin, t2 * cos + t1 * sin], axis=-1)

    def body(c, carry):
        rows = pl.ds(pl.multiple_of(c * RET_CHUNK, RET_CHUNK), RET_CHUNK)
        cos, sin = cos_ref[rows, :], sin_ref[rows, :]
        q = rope(q_ref[rows, :].astype(F32), cos, sin)
        k = rope(k_ref[rows, :].astype(F32), cos, sin) * scale
        v = v_ref[rows, :]
        qb = q.astype(BF16)
        scores = lax.dot_general(qb, k.astype(BF16), (((1,), (1,)), ((), ())),
                                 preferred_element_type=F32) * dec_ref[0]
        out = jnp.dot(scores.astype(BF16), v, preferred_element_type=F32)
        out = out + jnp.dot(qb, r_ref[...].astype(BF16), preferred_element_type=F32) * xi_ref[0]
        kz = (k * zeta_ref[0]).T.astype(BF16)
        r_ref[...] = r_ref[...] * gam_ref[0] + jnp.dot(kz, v, preferred_element_type=F32)
        ms = jnp.mean(out * out, axis=-1, keepdims=True)
        g = g_ref[rows, :].astype(F32)
        out = out * lax.rsqrt(ms + EPS) * nw_ref[...] * (g * _sigmoid(g))
        o_ref[rows, :] = out.astype(o_ref.dtype)
        return carry

    lax.fori_loop(0, n_chunks, body, 0, unroll=8)


def _retention(proj, ret_norm_w, tables, batch, seq, col0):
    t = proj.shape[0]
    d_ret = ret_norm_w.shape[0]
    hd = d_ret // RET_HEADS
    cos, sin, decay_in, zeta, xi, gamma_row = tables
    n_chunks = seq // RET_CHUNK

    def col(group):
        return pl.BlockSpec((seq, hd), lambda b, h: (b, col0 + group * RET_HEADS + h))

    per_head = lambda a, b_: pl.BlockSpec((1, a, b_), lambda b, h: (h, 0, 0))
    const2 = lambda a, b_: pl.BlockSpec((a, b_), lambda b, h: (0, 0))
    return pl.pallas_call(
        functools.partial(_retention_kernel, n_chunks=n_chunks, head_dim=hd),
        grid=(batch, RET_HEADS),
        in_specs=[col(0), col(1), col(2), col(3),
                  const2(seq, hd // 2), const2(seq, hd // 2),
                  per_head(RET_CHUNK, RET_CHUNK), per_head(RET_CHUNK, hd), per_head(RET_CHUNK, hd),
                  per_head(1, hd),
                  pl.BlockSpec((1, hd), lambda b, h: (0, h))],
        out_specs=pl.BlockSpec((seq, hd), lambda b, h: (b, h)),
        out_shape=jax.ShapeDtypeStruct((t, d_ret), BF16),
        scratch_shapes=[pltpu.VMEM((hd, hd), F32)],
        compiler_params=_params(("parallel", "parallel")),
    )(proj, proj, proj, proj, cos, sin, decay_in, zeta, xi, gamma_row, ret_norm_w.reshape(1, d_ret))


def _layer(x2, batch, seq, layer, norm_w, w_in, s5_mats, w_glu, b_glu, ssm_norm_w, ret_norm_w, w_out,
           ret_tables):
    t, d_model = x2.shape
    d_ssm = ssm_norm_w.shape[0]
    n_chunks = seq // S5_CHUNK
    n_rows = batch * n_chunks
    n_proj = w_in.shape[-1]

    h, h_slab = _rmsnorm_slab(x2, norm_w)
    proj = _inproj(h, w_in, layer, 2 * d_ssm, n_proj - 2 * d_ssm)
    gate_slab = _inproj(h_slab.reshape(t, d_model), w_in, layer, d_ssm, d_ssm)
    gate_slab = gate_slab.reshape(S5_CHUNK, n_rows, d_ssm)
    u_t = _inproj_t(h_slab, w_in, layer, d_ssm)
    y_t = _s5(u_t, s5_mats, layer, n_chunks)
    y_ssm = _glu(y_t, w_glu, layer, b_glu, ssm_norm_w, gate_slab)
    y_ret = _retention(proj, ret_norm_w, ret_tables, batch, seq, 0)
    return _outproj(y_ssm, y_ret, w_out, layer, x2)


def kernel(x, norm_w, w_in, ssm_lambda_re, ssm_lambda_im, ssm_b_re, ssm_b_im, ssm_c_re, ssm_c_im,
           ssm_d, ssm_log_dt, ssm_w_glu, ssm_b_glu, ssm_norm_w, ret_norm_w, w_out, final_norm_w):
    batch, seq, d_model = x.shape
    depth = norm_w.shape[0]
    d_ret = ret_norm_w.shape[1]
    ret_tables = _retention_tables(seq, d_ret // RET_HEADS)
    flat = lambda a: a.reshape((-1,) + a.shape[2:])
    s5_mats = _s5_matrices(flat(ssm_lambda_re), flat(ssm_lambda_im), flat(ssm_b_re), flat(ssm_b_im),
                           flat(ssm_c_re), flat(ssm_c_im), ssm_d.reshape(-1), ssm_log_dt.reshape(-1),
                           seq // S5_CHUNK)
    h = x.reshape(batch * seq, d_model)
    for layer in range(depth):
        h = _layer(h, batch, seq, layer, norm_w[layer], w_in, s5_mats,
                   ssm_w_glu, ssm_b_glu[layer], ssm_norm_w[layer], ret_norm_w[layer],
                   w_out, ret_tables)
    out = _rmsnorm(h, final_norm_w, F32)
    return out.reshape(batch, seq, d_model)
```

```python
import functools
import math

import jax
import jax.numpy as jnp
from jax import lax
from jax.experimental import pallas as pl
from jax.experimental.pallas import tpu as pltpu

F32 = jnp.float32
BF16 = jnp.bfloat16

EPS = 1e-6
SSM_GROUP = 16
RET_HEADS = 8
RET_CHUNK = 128
ROPE_BASE = 10000.0
S5_CHUNK = 16

VMEM_LIMIT = 56 * 1024 * 1024


def _params(sem):
    return pltpu.CompilerParams(dimension_semantics=sem, vmem_limit_bytes=VMEM_LIMIT)


def _sigmoid(x):
    return 0.5 * (1.0 + jnp.tanh(0.5 * x))


def _single(block_shape, index_map):
    return pl.BlockSpec(block_shape, index_map, pipeline_mode=pl.Buffered(1))


def _rmsnorm_kernel(x_ref, w_ref, o_ref):
    x = x_ref[...]
    ms = jnp.mean(x * x, axis=-1, keepdims=True)
    o_ref[...] = (x * lax.rsqrt(ms + EPS) * w_ref[...]).astype(o_ref.dtype)


def _rmsnorm(x, w, out_dtype, tm=512):
    t, d = x.shape
    tm = min(tm, t)
    return pl.pallas_call(
        _rmsnorm_kernel,
        grid=(t // tm,),
        in_specs=[pl.BlockSpec((tm, d), lambda i: (i, 0)),
                  pl.BlockSpec((1, d), lambda i: (0, 0))],
        out_specs=pl.BlockSpec((tm, d), lambda i: (i, 0)),
        out_shape=jax.ShapeDtypeStruct((t, d), out_dtype),
        compiler_params=_params(("parallel",)),
    )(x, w.reshape(1, d))


def _rmsnorm_slab_kernel(x_ref, w_ref, p_ref, o_ref, os_ref):
    x = x_ref[...]
    ms = jnp.mean(x * x, axis=-1, keepdims=True)
    y = (x * lax.rsqrt(ms + EPS) * w_ref[...]).astype(BF16)
    o_ref[...] = y
    ys = jnp.dot(p_ref[...], y, preferred_element_type=F32).astype(BF16)
    os_ref[...] = ys.reshape(os_ref.shape)


def _rmsnorm_slab(x, w, tm=512):
    t, d = x.shape
    tm = min(tm, t)
    cpt = tm // S5_CHUNK
    row = jnp.arange(tm)
    perm = (row[None, :] == ((row % cpt) * S5_CHUNK + row // cpt)[:, None]).astype(BF16)
    return pl.pallas_call(
        _rmsnorm_slab_kernel,
        grid=(t // tm,),
        in_specs=[pl.BlockSpec((tm, d), lambda i: (i, 0)),
                  pl.BlockSpec((1, d), lambda i: (0, 0)),
                  pl.BlockSpec((tm, tm), lambda i: (0, 0))],
        out_specs=[pl.BlockSpec((tm, d), lambda i: (i, 0)),
                   pl.BlockSpec((S5_CHUNK, cpt, d), lambda i: (0, i, 0))],
        out_shape=[jax.ShapeDtypeStruct((t, d), BF16),
                   jax.ShapeDtypeStruct((S5_CHUNK, t // S5_CHUNK, d), BF16)],
        compiler_params=_params(("parallel",)),
    )(x, w.reshape(1, d), perm)


def _inproj_kernel(a_ref, w_ref, o_ref, wb_ref):
    @pl.when(pl.program_id(1) == 0)
    def _():
        wb_ref[...] = w_ref[...].astype(BF16)

    o_ref[...] = jnp.dot(a_ref[...], wb_ref[...], preferred_element_type=F32).astype(o_ref.dtype)


def _inproj(h, w_in, layer, col_start, n_cols, tm=1024, tn=1024):
    t, k = h.shape
    tm, tn = min(tm, t), min(tn, n_cols)
    assert col_start % tn == 0
    c0 = col_start // tn
    return pl.pallas_call(
        _inproj_kernel,
        grid=(n_cols // tn, t // tm),
        in_specs=[pl.BlockSpec((tm, k), lambda j, i: (i, 0)),
                  _single((None, k, tn), lambda j, i: (layer, 0, c0 + j))],
        out_specs=pl.BlockSpec((tm, tn), lambda j, i: (i, j)),
        out_shape=jax.ShapeDtypeStruct((t, n_cols), BF16),
        scratch_shapes=[pltpu.VMEM((k, tn), BF16)],
        compiler_params=_params(("arbitrary", "arbitrary")),
    )(h, w_in)


def _inproj_t_kernel(h_ref, w_ref, o_ref, wt_ref):
    @pl.when(pl.program_id(1) == 0)
    def _():
        k, tc = w_ref.shape
        for k0 in range(0, k, tc):
            wt_ref[:, k0:k0 + tc] = w_ref[k0:k0 + tc, :].T.astype(BF16)

    acc = lax.dot_general(wt_ref[...], h_ref[...], (((1,), (1,)), ((), ())),
                          preferred_element_type=F32)
    o_ref[...] = acc.astype(BF16).reshape(o_ref.shape)


def _inproj_t(h_slab, w_in, layer, n_ch, tc=1024):
    n_slabs, n_rows, k = h_slab.shape
    tc = min(tc, n_ch)
    return pl.pallas_call(
        _inproj_t_kernel,
        grid=(n_ch // tc, n_slabs),
        in_specs=[pl.BlockSpec((None, n_rows, k), lambda c, j: (j, 0, 0)),
                  _single((None, k, tc), lambda c, j: (layer, 0, c))],
        out_specs=pl.BlockSpec((tc // SSM_GROUP, SSM_GROUP, n_rows), lambda c, j: (c, j, 0)),
        out_shape=jax.ShapeDtypeStruct((n_ch // SSM_GROUP, n_slabs * SSM_GROUP, n_rows), BF16),
        scratch_shapes=[pltpu.VMEM((tc, k), BF16)],
        compiler_params=_params(("arbitrary", "arbitrary")),
    )(h_slab, w_in)


def _outproj_kernel(ys_ref, yr_ref, ws_ref, wr_ref, x_ref, o_ref, wsb_ref, wrb_ref):
    @pl.when(pl.program_id(1) == 0)
    def _():
        wsb_ref[...] = ws_ref[...].astype(BF16)
        wrb_ref[...] = wr_ref[...].astype(BF16)

    acc = jnp.dot(ys_ref[...], wsb_ref[...], preferred_element_type=F32)
    acc = acc + jnp.dot(yr_ref[...], wrb_ref[...], preferred_element_type=F32)
    o_ref[...] = x_ref[...] + acc


def _outproj(y_ssm, y_ret, w_out, layer, x, tm=512, tn=1024):
    t, ds = y_ssm.shape
    _, dr = y_ret.shape
    assert ds == dr
    n = w_out.shape[-1]
    tm, tn = min(tm, t), min(tn, n)
    return pl.pallas_call(
        _outproj_kernel,
        grid=(n // tn, t // tm),
        in_specs=[pl.BlockSpec((tm, ds), lambda j, i: (i, 0)),
                  pl.BlockSpec((tm, dr), lambda j, i: (i, 0)),
                  _single((None, ds, tn), lambda j, i: (layer, 0, j)),
                  _single((None, dr, tn), lambda j, i: (layer, 1, j)),
                  pl.BlockSpec((tm, tn), lambda j, i: (i, j))],
        out_specs=pl.BlockSpec((tm, tn), lambda j, i: (i, j)),
        out_shape=jax.ShapeDtypeStruct((t, n), F32),
        scratch_shapes=[pltpu.VMEM((ds, tn), BF16), pltpu.VMEM((dr, tn), BF16)],
        compiler_params=_params(("arbitrary", "arbitrary")),
    )(y_ssm, y_ret, w_out, w_out, x)


def _s5_matrices(lam_re, lam_im, b_re, b_im, c_re, c_im, d_skip, log_dt, n_chunks):
    g, n = lam_re.shape
    h = SSM_GROUP
    c = S5_CHUNK
    hi = lax.Precision.HIGHEST
    dt = jnp.exp(log_dt)[:, None]
    tau = jnp.arange(c + 1, dtype=F32)[:, None, None]
    mag = jnp.exp(tau * (lam_re * dt)[None])
    ang = tau * (lam_im * dt)[None]
    p_re, p_im = mag * jnp.cos(ang), mag * jnp.sin(ang)
    ab_re, ab_im = p_re[1], p_im[1]
    num_re, num_im = ab_re - 1.0, ab_im
    den = lam_re * lam_re + lam_im * lam_im
    co_re = (num_re * lam_re + num_im * lam_im) / den
    co_im = (num_im * lam_re - num_re * lam_im) / den
    bb_re = co_re[..., None] * b_re - co_im[..., None] * b_im
    bb_im = co_re[..., None] * b_im + co_im[..., None] * b_re
    col = jnp.arange(c * h)
    spread_j = (col[None, :] // h == jnp.arange(c)[:, None]).astype(F32)
    spread_h = (col[None, :] % h == jnp.arange(h)[:, None]).astype(F32)
    q_re = jnp.einsum('jgn,jp->gnp', p_re[:c][::-1], spread_j, precision=hi)
    q_im = jnp.einsum('jgn,jp->gnp', p_im[:c][::-1], spread_j, precision=hi)
    x_re = jnp.einsum('gnk,kp->gnp', bb_re, spread_h, precision=hi)
    x_im = jnp.einsum('gnk,kp->gnp', bb_im, spread_h, precision=hi)
    e_t = jnp.concatenate([q_re * x_re - q_im * x_im, q_re * x_im + q_im * x_re], axis=1)
    c_a = jnp.concatenate([c_re, -c_im], axis=-1)
    c_b = jnp.concatenate([-c_im, -c_re], axis=-1)
    pt_re = p_re[1:].transpose(1, 0, 2)
    pt_im = p_im[1:].transpose(1, 0, 2)
    p_tab = jnp.concatenate([jnp.concatenate([pt_re, pt_re], axis=-1),
                             jnp.concatenate([pt_im, pt_im], axis=-1)], axis=1)
    n_steps = max(1, int(math.log2(n_chunks)))
    assert n_steps <= 8
    ar, ai = p_re[c], p_im[c]
    cols_re, cols_im = [], []
    for _ in range(n_steps):
        cols_re.append(ar)
        cols_im.append(ai)
        ar, ai = ar * ar - ai * ai, 2.0 * ar * ai
    pad = [jnp.zeros((g, n), F32)] * (8 - n_steps)
    stacked = jnp.stack(cols_re + pad + cols_im + pad, axis=0)
    a_pow = jnp.einsum('kgn,km->gnm', stacked, jnp.eye(16, 128, dtype=F32), precision=hi)
    d_col = jnp.tile(d_skip.reshape(g, 1, h), (1, c, 1)).reshape(g, c * h, 1)
    return e_t.astype(BF16), c_a, c_b, p_tab, a_pow, d_col


def _s5_group(u, e, c_a, c_b, p_tab, ap, d_col, n_chunks):
    gsz = c_a.shape[0]
    p = e.shape[1]
    n_pos = p // gsz
    k_rev = jnp.dot(c_a.astype(BF16), e, preferred_element_type=F32)
    lane = lax.broadcasted_iota(jnp.int32, k_rev.shape, 1)
    blocks = []
    for i in range(n_pos):
        shift = (n_pos - 1 - i) * gsz
        if shift == 0:
            blocks.append(k_rev)
        else:
            blocks.append(jnp.where(lane < p - shift, pltpu.roll(k_rev, p - shift, axis=1), 0.0))
    m_t = jnp.concatenate(blocks, axis=0).astype(BF16)
    f_t = jnp.concatenate([c_a * p_tab[i:i + 1] + c_b * p_tab[n_pos + i:n_pos + i + 1]
                           for i in range(n_pos)], axis=0).astype(BF16)

    y = jnp.dot(m_t, u, preferred_element_type=F32)
    s = jnp.dot(e, u, preferred_element_type=F32)
    n = s.shape[0] // 2
    cols = s.shape[1]
    chunk = lax.broadcasted_iota(jnp.int32, (n, cols), 1) % n_chunks
    z_re, z_im = s[:n], s[n:]
    step = 1
    m = 0
    while step < n_chunks:
        keep = chunk >= step
        s_re = jnp.where(keep, pltpu.roll(z_re, step, axis=1), 0.0)
        s_im = jnp.where(keep, pltpu.roll(z_im, step, axis=1), 0.0)
        a_re, a_im = ap[:, m:m + 1], ap[:, 8 + m:9 + m]
        z_re, z_im = z_re + a_re * s_re - a_im * s_im, z_im + a_re * s_im + a_im * s_re
        step *= 2
        m += 1
    first = chunk >= 1
    prev = jnp.concatenate([jnp.where(first, pltpu.roll(z_re, 1, axis=1), 0.0),
                            jnp.where(first, pltpu.roll(z_im, 1, axis=1), 0.0)], axis=0)
    y = y + jnp.dot(f_t, prev.astype(BF16), preferred_element_type=F32)
    return y + u.astype(F32) * d_col


def _s5_kernel(u_ref, e_ref, ca_ref, cb_ref, pt_ref, ap_ref, d_ref, y_ref, *, n_chunks):
    for gi in range(u_ref.shape[0]):
        y = _s5_group(u_ref[gi], e_ref[gi], ca_ref[gi], cb_ref[gi], pt_ref[gi], ap_ref[gi], d_ref[gi],
                      n_chunks)
        y_ref[gi] = y.astype(y_ref.dtype)


def _s5(u_t, mats, layer, n_chunks, gb=16):
    g, p, r = u_t.shape
    e_t, c_a, c_b, p_tab, a_pow, d_col = mats
    n2 = e_t.shape[1]
    gb = min(gb, g)
    steps = g // gb
    spec3 = lambda a, b: pl.BlockSpec((gb, a, b), lambda i: (i, 0, 0))
    tab3 = lambda a, b: pl.BlockSpec((gb, a, b), lambda i: (layer * steps + i, 0, 0))
    return pl.pallas_call(
        functools.partial(_s5_kernel, n_chunks=n_chunks),
        grid=(steps,),
        in_specs=[spec3(p, r), tab3(n2, p), tab3(SSM_GROUP, n2), tab3(SSM_GROUP, n2),
                  tab3(2 * S5_CHUNK, n2), tab3(n2 // 2, 128), tab3(p, 1)],
        out_specs=spec3(p, r),
        out_shape=jax.ShapeDtypeStruct((g, p, r), BF16),
        compiler_params=_params(("parallel",)),
    )(u_t, e_t, c_a, c_b, p_tab, a_pow, d_col)


def _glu_kernel(yt_ref, w_ref, b_ref, nw_ref, g_ref, o_ref, wb_ref):
    @pl.when((pl.program_id(0) == 0) & (pl.program_id(1) == 0))
    def _():
        wb_ref[...] = w_ref[...].astype(BF16)

    groups, gsz, rows = yt_ref.shape
    y = yt_ref[...].reshape(groups * gsz, rows).astype(F32).T
    y = y * (0.5 * (1.0 + jnp.tanh(math.sqrt(2.0 / math.pi) * (y + 0.044715 * (y * y * y)))))
    z = jnp.dot(y.astype(BF16), wb_ref[...], preferred_element_type=F32) + b_ref[...]
    y = y * _sigmoid(z)
    ms = jnp.mean(y * y, axis=-1, keepdims=True)
    y = y * lax.rsqrt(ms + EPS) * nw_ref[...]
    g = g_ref[...].astype(F32)
    o_ref[...] = (y * (g * _sigmoid(g))).astype(o_ref.dtype)


def _glu(y_t, w_glu, layer, b_glu, norm_w, gate_slab, tr=256):
    groups, _, n_rows = y_t.shape
    d = groups * SSM_GROUP
    tr = min(tr, n_rows)
    out = pl.pallas_call(
        _glu_kernel,
        grid=(S5_CHUNK, n_rows // tr),
        in_specs=[pl.BlockSpec((groups, SSM_GROUP, tr), lambda i, r: (0, i, r)),
                  _single((None, d, d), lambda i, r: (layer, 0, 0)),
                  pl.BlockSpec((1, d), lambda i, r: (0, 0)),
                  pl.BlockSpec((1, d), lambda i, r: (0, 0)),
                  pl.BlockSpec((None, tr, d), lambda i, r: (i, r, 0))],
        out_specs=pl.BlockSpec((None, tr, d), lambda i, r: (i, r, 0)),
        out_shape=jax.ShapeDtypeStruct((S5_CHUNK, n_rows, d), BF16),
        scratch_shapes=[pltpu.VMEM((d, d), BF16)],
        compiler_params=_params(("arbitrary", "arbitrary")),
    )(y_t, w_glu, b_glu.reshape(1, d), norm_w.reshape(1, d), gate_slab)
    return _unslab(out)


def _unslab_kernel(p_ref, y_ref, o_ref):
    n_slabs, cpt, d = y_ref.shape
    y = y_ref[...].reshape(n_slabs * cpt, d)
    o_ref[...] = jnp.dot(p_ref[...], y, preferred_element_type=F32).astype(o_ref.dtype)


def _unslab(y_slab, tm=512):
    n_slabs, n_rows, d = y_slab.shape
    t = n_slabs * n_rows
    tm = min(tm, t)
    cpt = tm // n_slabs
    row = jnp.arange(tm)
    perm_t = (row[None, :] == ((row % n_slabs) * cpt + row // n_slabs)[:, None]).astype(BF16)
    return pl.pallas_call(
        _unslab_kernel,
        grid=(t // tm,),
        in_specs=[pl.BlockSpec((tm, tm), lambda i: (0, 0)),
                  pl.BlockSpec((n_slabs, cpt, d), lambda i: (0, i, 0))],
        out_specs=pl.BlockSpec((tm, d), lambda i: (i, 0)),
        out_shape=jax.ShapeDtypeStruct((t, d), y_slab.dtype),
        compiler_params=_params(("parallel",)),
    )(perm_t, y_slab)


def _retention_tables(seq, head_dim):
    pos = jnp.arange(seq, dtype=F32)
    inv_freq = ROPE_BASE ** (-jnp.arange(0, head_dim, 2, dtype=F32) / head_dim)
    ang = pos[:, None] * inv_freq[None, :]
    cos, sin = jnp.cos(ang), jnp.sin(ang)
    log_gamma = jnp.log1p(-jnp.power(2.0, -5.0 - jnp.arange(RET_HEADS, dtype=F32)))
    idx = jnp.arange(RET_CHUNK, dtype=F32)
    rel = idx[:, None] - idx[None, :]
    causal = rel >= 0
    decay_in = jnp.where(causal[None],
                         jnp.exp(jnp.where(causal, rel, 0.0)[None] * log_gamma[:, None, None]),
                         0.0)
    zeta = jnp.exp((RET_CHUNK - 1.0 - idx)[None, :] * log_gamma[:, None])
    xi = jnp.exp((idx + 1.0)[None, :] * log_gamma[:, None])
    gamma_c = jnp.exp(RET_CHUNK * log_gamma)
    bcast = lambda a: jnp.broadcast_to(a[:, :, None], a.shape + (head_dim,))
    gamma_row = jnp.broadcast_to(gamma_c[:, None, None], (RET_HEADS, 1, head_dim))
    return cos, sin, decay_in, bcast(zeta), bcast(xi), gamma_row


def _retention_kernel(q_ref, k_ref, v_ref, g_ref, cos_ref, sin_ref, dec_ref, zeta_ref, xi_ref,
                      gam_ref, nw_ref, o_ref, r_ref, *, n_chunks, head_dim):
    half = head_dim // 2
    scale = head_dim ** -0.5
    r_ref[...] = jnp.zeros_like(r_ref)

    def rope(t, cos, sin):
        t1, t2 = t[:, :half], t[:, half:]
        return jnp.concatenate([t1 * cos - t2 * sin, t2 * cos + t1 * sin], axis=-1)

    def body(c, carry):
        rows = pl.ds(pl.multiple_of(c * RET_CHUNK, RET_CHUNK), RET_CHUNK)
        cos, sin = cos_ref[rows, :], sin_ref[rows, :]
        q = rope(q_ref[rows, :].astype(F32), cos, sin)
        k = rope(k_ref[rows, :].astype(F32), cos, sin) * scale
        v = v_ref[rows, :]
        qb = q.astype(BF16)
        scores = lax.dot_general(qb, k.astype(BF16), (((1,), (1,)), ((), ())),
                                 preferred_element_type=F32) * dec_ref[0]
        out = jnp.dot(scores.astype(BF16), v, preferred_element_type=F32)
        out = out + jnp.dot(qb, r_ref[...].astype(BF16), preferred_element_type=F32) * xi_ref[0]
        kz = (k * zeta_ref[0]).T.astype(BF16)
        r_ref[...] = r_ref[...] * gam_ref[0] + jnp.dot(kz, v, preferred_element_type=F32)
        ms = jnp.mean(out * out, axis=-1, keepdims=True)
        g = g_ref[rows, :].astype(F32)
        out = out * lax.rsqrt(ms + EPS) * nw_ref[...] * (g * _sigmoid(g))
        o_ref[rows, :] = out.astype(o_ref.dtype)
        return carry

    lax.fori_loop(0, n_chunks, body, 0, unroll=True)


def _retention(proj, ret_norm_w, tables, batch, seq, col0):
    t = proj.shape[0]
    d_ret = ret_norm_w.shape[0]
    hd = d_ret // RET_HEADS
    cos, sin, decay_in, zeta, xi, gamma_row = tables
    n_chunks = seq // RET_CHUNK

    def col(group):
        return pl.BlockSpec((seq, hd), lambda b, h: (b, col0 + group * RET_HEADS + h))

    per_head = lambda a, b_: pl.BlockSpec((1, a, b_), lambda b, h: (h, 0, 0))
    const2 = lambda a, b_: pl.BlockSpec((a, b_), lambda b, h: (0, 0))
    return pl.pallas_call(
        functools.partial(_retention_kernel, n_chunks=n_chunks, head_dim=hd),
        grid=(batch, RET_HEADS),
        in_specs=[col(0), col(1), col(2), col(3),
                  const2(seq, hd // 2), const2(seq, hd // 2),
                  per_head(RET_CHUNK, RET_CHUNK), per_head(RET_CHUNK, hd), per_head(RET_CHUNK, hd),
                  per_head(1, hd),
                  pl.BlockSpec((1, hd), lambda b, h: (0, h))],
        out_specs=pl.BlockSpec((seq, hd), lambda b, h: (b, h)),
        out_shape=jax.ShapeDtypeStruct((t, d_ret), BF16),
        scratch_shapes=[pltpu.VMEM((hd, hd), F32)],
        compiler_params=_params(("parallel", "parallel")),
    )(proj, proj, proj, proj, cos, sin, decay_in, zeta, xi, gamma_row, ret_norm_w.reshape(1, d_ret))


def _layer(x2, batch, seq, layer, norm_w, w_in, s5_mats, w_glu, b_glu, ssm_norm_w, ret_norm_w, w_out,
           ret_tables):
    t, d_model = x2.shape
    d_ssm = ssm_norm_w.shape[0]
    n_chunks = seq // S5_CHUNK
    n_rows = batch * n_chunks
    n_proj = w_in.shape[-1]

    h, h_slab = _rmsnorm_slab(x2, norm_w)
    proj = _inproj(h, w_in, layer, 2 * d_ssm, n_proj - 2 * d_ssm)
    gate_slab = _inproj(h_slab.reshape(t, d_model), w_in, layer, d_ssm, d_ssm)
    gate_slab = gate_slab.reshape(S5_CHUNK, n_rows, d_ssm)
    u_t = _inproj_t(h_slab, w_in, layer, d_ssm)
    y_t = _s5(u_t, s5_mats, layer, n_chunks)
    y_ssm = _glu(y_t, w_glu, layer, b_glu, ssm_norm_w, gate_slab)
    y_ret = _retention(proj, ret_norm_w, ret_tables, batch, seq, 0)
    return _outproj(y_ssm, y_ret, w_out, layer, x2)


def kernel(x, norm_w, w_in, ssm_lambda_re, ssm_lambda_im, ssm_b_re, ssm_b_im, ssm_c_re, ssm_c_im,
           ssm_d, ssm_log_dt, ssm_w_glu, ssm_b_glu, ssm_norm_w, ret_norm_w, w_out, final_norm_w):
    batch, seq, d_model = x.shape
    depth = norm_w.shape[0]
    d_ret = ret_norm_w.shape[1]
    ret_tables = _retention_tables(seq, d_ret // RET_HEADS)
    flat = lambda a: a.reshape((-1,) + a.shape[2:])
    s5_mats = _s5_matrices(flat(ssm_lambda_re), flat(ssm_lambda_im), flat(ssm_b_re), flat(ssm_b_im),
                           flat(ssm_c_re), flat(ssm_c_im), ssm_d.reshape(-1), ssm_log_dt.reshape(-1),
                           seq // S5_CHUNK)
    h = x.reshape(batch * seq, d_model)
    for layer in range(depth):
        h = _layer(h, batch, seq, layer, norm_w[layer], w_in, s5_mats,
                   ssm_w_glu, ssm_b_glu[layer], ssm_norm_w[layer], ret_norm_w[layer],
                   w_out, ret_tables)
    out = _rmsnorm(h, final_norm_w, F32)
    return out.reshape(batch, seq, d_model)
```

```python
import functools
import math

import jax
import jax.numpy as jnp
from jax import lax
from jax.experimental import pallas as pl
from jax.experimental.pallas import tpu as pltpu

F32 = jnp.float32
BF16 = jnp.bfloat16

EPS = 1e-6
SSM_GROUP = 16
RET_HEADS = 8
RET_CHUNK = 128
ROPE_BASE = 10000.0
S5_CHUNK = 16

VMEM_LIMIT = 56 * 1024 * 1024


def _params(sem):
    return pltpu.CompilerParams(dimension_semantics=sem, vmem_limit_bytes=VMEM_LIMIT)


def _sigmoid(x):
    return 0.5 * (1.0 + jnp.tanh(0.5 * x))


def _single(block_shape, index_map):
    return pl.BlockSpec(block_shape, index_map, pipeline_mode=pl.Buffered(1))


def _rmsnorm_kernel(x_ref, w_ref, o_ref):
    x = x_ref[...]
    ms = jnp.mean(x * x, axis=-1, keepdims=True)
    o_ref[...] = (x * lax.rsqrt(ms + EPS) * w_ref[...]).astype(o_ref.dtype)


def _rmsnorm(x, w, out_dtype, tm=512):
    t, d = x.shape
    tm = min(tm, t)
    return pl.pallas_call(
        _rmsnorm_kernel,
        grid=(t // tm,),
        in_specs=[pl.BlockSpec((tm, d), lambda i: (i, 0)),
                  pl.BlockSpec((1, d), lambda i: (0, 0))],
        out_specs=pl.BlockSpec((tm, d), lambda i: (i, 0)),
        out_shape=jax.ShapeDtypeStruct((t, d), out_dtype),
        compiler_params=_params(("parallel",)),
    )(x, w.reshape(1, d))


def _rmsnorm_slab_kernel(x_ref, w_ref, p_ref, o_ref, os_ref):
    x = x_ref[...]
    ms = jnp.mean(x * x, axis=-1, keepdims=True)
    y = (x * lax.rsqrt(ms + EPS) * w_ref[...]).astype(BF16)
    o_ref[...] = y
    ys = jnp.dot(p_ref[...], y, preferred_element_type=F32).astype(BF16)
    os_ref[...] = ys.reshape(os_ref.shape)


def _rmsnorm_slab(x, w, tm=512):
    t, d = x.shape
    tm = min(tm, t)
    cpt = tm // S5_CHUNK
    row = jnp.arange(tm)
    perm = (row[None, :] == ((row % cpt) * S5_CHUNK + row // cpt)[:, None]).astype(BF16)
    return pl.pallas_call(
        _rmsnorm_slab_kernel,
        grid=(t // tm,),
        in_specs=[pl.BlockSpec((tm, d), lambda i: (i, 0)),
                  pl.BlockSpec((1, d), lambda i: (0, 0)),
                  pl.BlockSpec((tm, tm), lambda i: (0, 0))],
        out_specs=[pl.BlockSpec((tm, d), lambda i: (i, 0)),
                   pl.BlockSpec((S5_CHUNK, cpt, d), lambda i: (0, i, 0))],
        out_shape=[jax.ShapeDtypeStruct((t, d), BF16),
                   jax.ShapeDtypeStruct((S5_CHUNK, t // S5_CHUNK, d), BF16)],
        compiler_params=_params(("parallel",)),
    )(x, w.reshape(1, d), perm)


def _inproj_kernel(a_ref, w_ref, o_ref, wb_ref):
    @pl.when(pl.program_id(1) == 0)
    def _():
        wb_ref[...] = w_ref[...].astype(BF16)

    o_ref[...] = jnp.dot(a_ref[...], wb_ref[...], preferred_element_type=F32).astype(o_ref.dtype)


def _inproj(h, w_in, layer, col_start, n_cols, tm=1024, tn=1024, prefetch_w=False):
    t, k = h.shape
    tm, tn = min(tm, t), min(tn, n_cols)
    assert col_start % tn == 0
    c0 = col_start // tn
    w_spec = pl.BlockSpec if prefetch_w else _single
    return pl.pallas_call(
        _inproj_kernel,
        grid=(n_cols // tn, t // tm),
        in_specs=[pl.BlockSpec((tm, k), lambda j, i: (i, 0)),
                  w_spec((None, k, tn), lambda j, i: (layer, 0, c0 + j))],
        out_specs=pl.BlockSpec((tm, tn), lambda j, i: (i, j)),
        out_shape=jax.ShapeDtypeStruct((t, n_cols), BF16),
        scratch_shapes=[pltpu.VMEM((k, tn), BF16)],
        compiler_params=_params(("arbitrary", "arbitrary")),
    )(h, w_in)


def _inproj_t_kernel(h_ref, w_ref, o_ref, wt_ref):
    @pl.when(pl.program_id(1) == 0)
    def _():
        k, tc = w_ref.shape
        for k0 in range(0, k, tc):
            wt_ref[:, k0:k0 + tc] = w_ref[k0:k0 + tc, :].T.astype(BF16)

    acc = lax.dot_general(wt_ref[...], h_ref[...], (((1,), (1,)), ((), ())),
                          preferred_element_type=F32)
    o_ref[...] = acc.astype(BF16).reshape(o_ref.shape)


def _inproj_t(h_slab, w_in, layer, n_ch, tc=1024):
    n_slabs, n_rows, k = h_slab.shape
    tc = min(tc, n_ch)
    return pl.pallas_call(
        _inproj_t_kernel,
        grid=(n_ch // tc, n_slabs),
        in_specs=[pl.BlockSpec((None, n_rows, k), lambda c, j: (j, 0, 0)),
                  _single((None, k, tc), lambda c, j: (layer, 0, c))],
        out_specs=pl.BlockSpec((tc // SSM_GROUP, SSM_GROUP, n_rows), lambda c, j: (c, j, 0)),
        out_shape=jax.ShapeDtypeStruct((n_ch // SSM_GROUP, n_slabs * SSM_GROUP, n_rows), BF16),
        scratch_shapes=[pltpu.VMEM((tc, k), BF16)],
        compiler_params=_params(("arbitrary", "arbitrary")),
    )(h_slab, w_in)


def _outproj_kernel(ys_ref, yr_ref, ws_ref, wr_ref, x_ref, o_ref, wsb_ref, wrb_ref):
    @pl.when(pl.program_id(1) == 0)
    def _():
        wsb_ref[...] = ws_ref[...].astype(BF16)
        wrb_ref[...] = wr_ref[...].astype(BF16)

    acc = jnp.dot(ys_ref[...], wsb_ref[...], preferred_element_type=F32)
    acc = acc + jnp.dot(yr_ref[...], wrb_ref[...], preferred_element_type=F32)
    o_ref[...] = x_ref[...] + acc


def _outproj(y_ssm, y_ret, w_out, layer, x, tm=512, tn=1024):
    t, ds = y_ssm.shape
    _, dr = y_ret.shape
    assert ds == dr
    n = w_out.shape[-1]
    tm, tn = min(tm, t), min(tn, n)
    return pl.pallas_call(
        _outproj_kernel,
        grid=(n // tn, t // tm),
        in_specs=[pl.BlockSpec((tm, ds), lambda j, i: (i, 0)),
                  pl.BlockSpec((tm, dr), lambda j, i: (i, 0)),
                  _single((None, ds, tn), lambda j, i: (layer, 0, j)),
                  _single((None, dr, tn), lambda j, i: (layer, 1, j)),
                  pl.BlockSpec((tm, tn), lambda j, i: (i, j))],
        out_specs=pl.BlockSpec((tm, tn), lambda j, i: (i, j)),
        out_shape=jax.ShapeDtypeStruct((t, n), F32),
        scratch_shapes=[pltpu.VMEM((ds, tn), BF16), pltpu.VMEM((dr, tn), BF16)],
        compiler_params=_params(("arbitrary", "arbitrary")),
    )(y_ssm, y_ret, w_out, w_out, x)


def _s5_matrices(lam_re, lam_im, b_re, b_im, c_re, c_im, d_skip, log_dt, n_chunks):
    g, n = lam_re.shape
    h = SSM_GROUP
    c = S5_CHUNK
    hi = lax.Precision.HIGHEST
    dt = jnp.exp(log_dt)[:, None]
    tau = jnp.arange(c + 1, dtype=F32)[:, None, None]
    mag = jnp.exp(tau * (lam_re * dt)[None])
    ang = tau * (lam_im * dt)[None]
    p_re, p_im = mag * jnp.cos(ang), mag * jnp.sin(ang)
    ab_re, ab_im = p_re[1], p_im[1]
    num_re, num_im = ab_re - 1.0, ab_im
    den = lam_re * lam_re + lam_im * lam_im
    co_re = (num_re * lam_re + num_im * lam_im) / den
    co_im = (num_im * lam_re - num_re * lam_im) / den
    bb_re = co_re[..., None] * b_re - co_im[..., None] * b_im
    bb_im = co_re[..., None] * b_im + co_im[..., None] * b_re
    col = jnp.arange(c * h)
    spread_j = (col[None, :] // h == jnp.arange(c)[:, None]).astype(F32)
    spread_h = (col[None, :] % h == jnp.arange(h)[:, None]).astype(F32)
    q_re = jnp.einsum('jgn,jp->gnp', p_re[:c][::-1], spread_j, precision=hi)
    q_im = jnp.einsum('jgn,jp->gnp', p_im[:c][::-1], spread_j, precision=hi)
    x_re = jnp.einsum('gnk,kp->gnp', bb_re, spread_h, precision=hi)
    x_im = jnp.einsum('gnk,kp->gnp', bb_im, spread_h, precision=hi)
    e_t = jnp.concatenate([q_re * x_re - q_im * x_im, q_re * x_im + q_im * x_re], axis=1)
    c_a = jnp.concatenate([c_re, -c_im], axis=-1)
    c_b = jnp.concatenate([-c_im, -c_re], axis=-1)
    pt_re = p_re[1:].transpose(1, 0, 2)
    pt_im = p_im[1:].transpose(1, 0, 2)
    p_tab = jnp.concatenate([jnp.concatenate([pt_re, pt_re], axis=-1),
                             jnp.concatenate([pt_im, pt_im], axis=-1)], axis=1)
    n_steps = max(1, int(math.log2(n_chunks)))
    assert n_steps <= 8
    ar, ai = p_re[c], p_im[c]
    cols_re, cols_im = [], []
    for _ in range(n_steps):
        cols_re.append(ar)
        cols_im.append(ai)
        ar, ai = ar * ar - ai * ai, 2.0 * ar * ai
    pad = [jnp.zeros((g, n), F32)] * (8 - n_steps)
    stacked = jnp.stack(cols_re + pad + cols_im + pad, axis=0)
    a_pow = jnp.einsum('kgn,km->gnm', stacked, jnp.eye(16, 128, dtype=F32), precision=hi)
    d_col = jnp.tile(d_skip.reshape(g, 1, h), (1, c, 1)).reshape(g, c * h, 1)
    return e_t.astype(BF16), c_a, c_b, p_tab, a_pow, d_col


def _s5_group(u, e, c_a, c_b, p_tab, ap, d_col, n_chunks):
    gsz = c_a.shape[0]
    p = e.shape[1]
    n_pos = p // gsz
    k_rev = jnp.dot(c_a.astype(BF16), e, preferred_element_type=F32)
    lane = lax.broadcasted_iota(jnp.int32, k_rev.shape, 1)
    blocks = []
    for i in range(n_pos):
        shift = (n_pos - 1 - i) * gsz
        if shift == 0:
            blocks.append(k_rev)
        else:
            blocks.append(jnp.where(lane < p - shift, pltpu.roll(k_rev, p - shift, axis=1), 0.0))
    m_t = jnp.concatenate(blocks, axis=0).astype(BF16)
    f_t = jnp.concatenate([c_a * p_tab[i:i + 1] + c_b * p_tab[n_pos + i:n_pos + i + 1]
                           for i in range(n_pos)], axis=0).astype(BF16)

    y = jnp.dot(m_t, u, preferred_element_type=F32)
    s = jnp.dot(e, u, preferred_element_type=F32)
    n = s.shape[0] // 2
    cols = s.shape[1]
    chunk = lax.broadcasted_iota(jnp.int32, (n, cols), 1) % n_chunks
    z_re, z_im = s[:n], s[n:]
    step = 1
    m = 0
    while step < n_chunks:
        keep = chunk >= step
        s_re = jnp.where(keep, pltpu.roll(z_re, step, axis=1), 0.0)
        s_im = jnp.where(keep, pltpu.roll(z_im, step, axis=1), 0.0)
        a_re, a_im = ap[:, m:m + 1], ap[:, 8 + m:9 + m]
        z_re, z_im = z_re + a_re * s_re - a_im * s_im, z_im + a_re * s_im + a_im * s_re
        step *= 2
        m += 1
    first = chunk >= 1
    prev = jnp.concatenate([jnp.where(first, pltpu.roll(z_re, 1, axis=1), 0.0),
                            jnp.where(first, pltpu.roll(z_im, 1, axis=1), 0.0)], axis=0)
    y = y + jnp.dot(f_t, prev.astype(BF16), preferred_element_type=F32)
    return y + u.astype(F32) * d_col


def _s5_kernel(u_ref, e_ref, ca_ref, cb_ref, pt_ref, ap_ref, d_ref, y_ref, *, n_chunks):
    for gi in range(u_ref.shape[0]):
        y = _s5_group(u_ref[gi], e_ref[gi], ca_ref[gi], cb_ref[gi], pt_ref[gi], ap_ref[gi], d_ref[gi],
                      n_chunks)
        y_ref[gi] = y.astype(y_ref.dtype)


def _s5(u_t, mats, layer, n_chunks, gb=16):
    g, p, r = u_t.shape
    e_t, c_a, c_b, p_tab, a_pow, d_col = mats
    n2 = e_t.shape[1]
    gb = min(gb, g)
    steps = g // gb
    spec3 = lambda a, b: pl.BlockSpec((gb, a, b), lambda i: (i, 0, 0))
    tab3 = lambda a, b: pl.BlockSpec((gb, a, b), lambda i: (layer * steps + i, 0, 0))
    return pl.pallas_call(
        functools.partial(_s5_kernel, n_chunks=n_chunks),
        grid=(steps,),
        in_specs=[spec3(p, r), tab3(n2, p), tab3(SSM_GROUP, n2), tab3(SSM_GROUP, n2),
                  tab3(2 * S5_CHUNK, n2), tab3(n2 // 2, 128), tab3(p, 1)],
        out_specs=spec3(p, r),
        out_shape=jax.ShapeDtypeStruct((g, p, r), BF16),
        compiler_params=_params(("parallel",)),
    )(u_t, e_t, c_a, c_b, p_tab, a_pow, d_col)


def _glu_kernel(yt_ref, w_ref, b_ref, nw_ref, g_ref, o_ref, wb_ref):
    @pl.when((pl.program_id(0) == 0) & (pl.program_id(1) == 0))
    def _():
        wb_ref[...] = w_ref[...].astype(BF16)

    groups, gsz, rows = yt_ref.shape
    y = yt_ref[...].reshape(groups * gsz, rows).astype(F32).T
    y = y * (0.5 * (1.0 + jnp.tanh(math.sqrt(2.0 / math.pi) * (y + 0.044715 * (y * y * y)))))
    z = jnp.dot(y.astype(BF16), wb_ref[...], preferred_element_type=F32) + b_ref[...]
    y = y * _sigmoid(z)
    ms = jnp.mean(y * y, axis=-1, keepdims=True)
    y = y * lax.rsqrt(ms + EPS) * nw_ref[...]
    g = g_ref[...].astype(F32)
    o_ref[...] = (y * (g * _sigmoid(g))).astype(o_ref.dtype)


def _glu(y_t, w_glu, layer, b_glu, norm_w, gate_slab, tr=256):
    groups, _, n_rows = y_t.shape
    d = groups * SSM_GROUP
    tr = min(tr, n_rows)
    out = pl.pallas_call(
        _glu_kernel,
        grid=(S5_CHUNK, n_rows // tr),
        in_specs=[pl.BlockSpec((groups, SSM_GROUP, tr), lambda i, r: (0, i, r)),
                  _single((None, d, d), lambda i, r: (layer, 0, 0)),
                  pl.BlockSpec((1, d), lambda i, r: (0, 0)),
                  pl.BlockSpec((1, d), lambda i, r: (0, 0)),
                  pl.BlockSpec((None, tr, d), lambda i, r: (i, r, 0))],
        out_specs=pl.BlockSpec((None, tr, d), lambda i, r: (i, r, 0)),
        out_shape=jax.ShapeDtypeStruct((S5_CHUNK, n_rows, d), BF16),
        scratch_shapes=[pltpu.VMEM((d, d), BF16)],
        compiler_params=_params(("arbitrary", "arbitrary")),
    )(y_t, w_glu, b_glu.reshape(1, d), norm_w.reshape(1, d), gate_slab)
    return _unslab(out)


def _unslab_kernel(p_ref, y_ref, o_ref):
    n_slabs, cpt, d = y_ref.shape
    y = y_ref[...].reshape(n_slabs * cpt, d)
    o_ref[...] = jnp.dot(p_ref[...], y, preferred_element_type=F32).astype(o_ref.dtype)


def _unslab(y_slab, tm=512):
    n_slabs, n_rows, d = y_slab.shape
    t = n_slabs * n_rows
    tm = min(tm, t)
    cpt = tm // n_slabs
    row = jnp.arange(tm)
    perm_t = (row[None, :] == ((row % n_slabs) * cpt + row // n_slabs)[:, None]).astype(BF16)
    return pl.pallas_call(
        _unslab_kernel,
        grid=(t // tm,),
        in_specs=[pl.BlockSpec((tm, tm), lambda i: (0, 0)),
                  pl.BlockSpec((n_slabs, cpt, d), lambda i: (0, i, 0))],
        out_specs=pl.BlockSpec((tm, d), lambda i: (i, 0)),
        out_shape=jax.ShapeDtypeStruct((t, d), y_slab.dtype),
        compiler_params=_params(("parallel",)),
    )(perm_t, y_slab)


def _retention_tables(seq, head_dim):
    pos = jnp.arange(seq, dtype=F32)
    inv_freq = ROPE_BASE ** (-jnp.arange(0, head_dim, 2, dtype=F32) / head_dim)
    ang = pos[:, None] * inv_freq[None, :]
    cos, sin = jnp.cos(ang), jnp.sin(ang)
    log_gamma = jnp.log1p(-jnp.power(2.0, -5.0 - jnp.arange(RET_HEADS, dtype=F32)))
    idx = jnp.arange(RET_CHUNK, dtype=F32)
    rel = idx[:, None] - idx[None, :]
    causal = rel >= 0
    decay_in = jnp.where(causal[None],
                         jnp.exp(jnp.where(causal, rel, 0.0)[None] * log_gamma[:, None, None]),
                         0.0)
    zeta = jnp.exp((RET_CHUNK - 1.0 - idx)[None, :] * log_gamma[:, None])
    xi = jnp.exp((idx + 1.0)[None, :] * log_gamma[:, None])
    gamma_c = jnp.exp(RET_CHUNK * log_gamma)
    bcast = lambda a: jnp.broadcast_to(a[:, :, None], a.shape + (head_dim,))
    gamma_row = jnp.broadcast_to(gamma_c[:, None, None], (RET_HEADS, 1, head_dim))
    return cos, sin, decay_in, bcast(zeta), bcast(xi), gamma_row


def _retention_kernel(q_ref, k_ref, v_ref, g_ref, cos_ref, sin_ref, dec_ref, zeta_ref, xi_ref,
                      gam_ref, nw_ref, o_ref, r_ref, *, n_chunks, head_dim):
    half = head_dim // 2
    scale = head_dim ** -0.5
    r_ref[...] = jnp.zeros_like(r_ref)

    def rope(t, cos, sin):
        t1, t2 = t[:, :half], t[:, half:]
        return jnp.concatenate([t1 * cos - t2 * sin, t2 * cos + t1 * sin], axis=-1)

    def body(c, carry):
        rows = pl.ds(pl.multiple_of(c * RET_CHUNK, RET_CHUNK), RET_CHUNK)
        cos, sin = cos_ref[rows, :], sin_ref[rows, :]
        q = rope(q_ref[rows, :].astype(F32), cos, sin)
        k = rope(k_ref[rows, :].astype(F32), cos, sin) * scale
        v = v_ref[rows, :]
        qb = q.astype(BF16)
        scores = lax.dot_general(qb, k.astype(BF16), (((1,), (1,)), ((), ())),
                                 preferred_element_type=F32) * dec_ref[0]
        out = jnp.dot(scores.astype(BF16), v, preferred_element_type=F32)
        out = out + jnp.dot(qb, r_ref[...].astype(BF16), preferred_element_type=F32) * xi_ref[0]
        kz = (k * zeta_ref[0]).T.astype(BF16)
        r_ref[...] = r_ref[...] * gam_ref[0] + jnp.dot(kz, v, preferred_element_type=F32)
        ms = jnp.mean(out * out, axis=-1, keepdims=True)
        g = g_ref[rows, :].astype(F32)
        out = out * lax.rsqrt(ms + EPS) * nw_ref[...] * (g * _sigmoid(g))
        o_ref[rows, :] = out.astype(o_ref.dtype)
        return carry

    lax.fori_loop(0, n_chunks, body, 0, unroll=True)


def _retention(proj, ret_norm_w, tables, batch, seq, col0):
    t = proj.shape[0]
    d_ret = ret_norm_w.shape[0]
    hd = d_ret // RET_HEADS
    cos, sin, decay_in, zeta, xi, gamma_row = tables
    n_chunks = seq // RET_CHUNK

    def col(group):
        return pl.BlockSpec((seq, hd), lambda b, h: (b, col0 + group * RET_HEADS + h))

    per_head = lambda a, b_: pl.BlockSpec((1, a, b_), lambda b, h: (h, 0, 0))
    const2 = lambda a, b_: pl.BlockSpec((a, b_), lambda b, h: (0, 0))
    return pl.pallas_call(
        functools.partial(_retention_kernel, n_chunks=n_chunks, head_dim=hd),
        grid=(batch, RET_HEADS),
        in_specs=[col(0), col(1), col(2), col(3),
                  const2(seq, hd // 2), const2(seq, hd // 2),
                  per_head(RET_CHUNK, RET_CHUNK), per_head(RET_CHUNK, hd), per_head(RET_CHUNK, hd),
                  per_head(1, hd),
                  pl.BlockSpec((1, hd), lambda b, h: (0, h))],
        out_specs=pl.BlockSpec((seq, hd), lambda b, h: (b, h)),
        out_shape=jax.ShapeDtypeStruct((t, d_ret), BF16),
        scratch_shapes=[pltpu.VMEM((hd, hd), F32)],
        compiler_params=_params(("parallel", "parallel")),
    )(proj, proj, proj, proj, cos, sin, decay_in, zeta, xi, gamma_row, ret_norm_w.reshape(1, d_ret))


def _layer(x2, batch, seq, layer, norm_w, w_in, s5_mats, w_glu, b_glu, ssm_norm_w, ret_norm_w, w_out,
           ret_tables):
    t, d_model = x2.shape
    d_ssm = ssm_norm_w.shape[0]
    n_chunks = seq // S5_CHUNK
    n_rows = batch * n_chunks
    n_proj = w_in.shape[-1]

    h, h_slab = _rmsnorm_slab(x2, norm_w)
    proj = _inproj(h, w_in, layer, 2 * d_ssm, n_proj - 2 * d_ssm, tm=512, prefetch_w=True)
    gate_slab = _inproj(h_slab.reshape(t, d_model), w_in, layer, d_ssm, d_ssm)
    gate_slab = gate_slab.reshape(S5_CHUNK, n_rows, d_ssm)
    u_t = _inproj_t(h_slab, w_in, layer, d_ssm)
    y_t = _s5(u_t, s5_mats, layer, n_chunks)
    y_ssm = _glu(y_t, w_glu, layer, b_glu, ssm_norm_w, gate_slab)
    y_ret = _retention(proj, ret_norm_w, ret_tables, batch, seq, 0)
    return _outproj(y_ssm, y_ret, w_out, layer, x2)


def kernel(x, norm_w, w_in, ssm_lambda_re, ssm_lambda_im, ssm_b_re, ssm_b_im, ssm_c_re, ssm_c_im,
           ssm_d, ssm_log_dt, ssm_w_glu, ssm_b_glu, ssm_norm_w, ret_norm_w, w_out, final_norm_w):
    batch, seq, d_model = x.shape
    depth = norm_w.shape[0]
    d_ret = ret_norm_w.shape[1]
    ret_tables = _retention_tables(seq, d_ret // RET_HEADS)
    flat = lambda a: a.reshape((-1,) + a.shape[2:])
    s5_mats = _s5_matrices(flat(ssm_lambda_re), flat(ssm_lambda_im), flat(ssm_b_re), flat(ssm_b_im),
                           flat(ssm_c_re), flat(ssm_c_im), ssm_d.reshape(-1), ssm_log_dt.reshape(-1),
                           seq // S5_CHUNK)
    h = x.reshape(batch * seq, d_model)
    for layer in range(depth):
        h = _layer(h, batch, seq, layer, norm_w[layer], w_in, s5_mats,
                   ssm_w_glu, ssm_b_glu[layer], ssm_norm_w[layer], ret_norm_w[layer],
                   w_out, ret_tables)
    out = _rmsnorm(h, final_norm_w, F32)
    return out.reshape(batch, seq, d_model)
```

```python
import functools
import math

import jax
import jax.numpy as jnp
from jax import lax
from jax.experimental import pallas as pl
from jax.experimental.pallas import tpu as pltpu

F32 = jnp.float32
BF16 = jnp.bfloat16

EPS = 1e-6
SSM_GROUP = 16
RET_HEADS = 8
RET_CHUNK = 128
ROPE_BASE = 10000.0
S5_CHUNK = 16

VMEM_LIMIT = 56 * 1024 * 1024


def _params(sem):
    return pltpu.CompilerParams(dimension_semantics=sem, vmem_limit_bytes=VMEM_LIMIT)


def _sigmoid(x):
    return 0.5 * (1.0 + jnp.tanh(0.5 * x))


def _single(block_shape, index_map):
    return pl.BlockSpec(block_shape, index_map, pipeline_mode=pl.Buffered(1))


def _rmsnorm_kernel(x_ref, w_ref, o_ref):
    x = x_ref[...]
    ms = jnp.mean(x * x, axis=-1, keepdims=True)
    o_ref[...] = (x * lax.rsqrt(ms + EPS) * w_ref[...]).astype(o_ref.dtype)


def _rmsnorm(x, w, out_dtype, tm=512):
    t, d = x.shape
    tm = min(tm, t)
    return pl.pallas_call(
        _rmsnorm_kernel,
        grid=(t // tm,),
        in_specs=[pl.BlockSpec((tm, d), lambda i: (i, 0)),
                  pl.BlockSpec((1, d), lambda i: (0, 0))],
        out_specs=pl.BlockSpec((tm, d), lambda i: (i, 0)),
        out_shape=jax.ShapeDtypeStruct((t, d), out_dtype),
        compiler_params=_params(("parallel",)),
    )(x, w.reshape(1, d))


def _rmsnorm_slab_kernel(x_ref, w_ref, p_ref, o_ref, os_ref):
    x = x_ref[...]
    ms = jnp.mean(x * x, axis=-1, keepdims=True)
    y = (x * lax.rsqrt(ms + EPS) * w_ref[...]).astype(BF16)
    o_ref[...] = y
    ys = jnp.dot(p_ref[...], y, preferred_element_type=F32).astype(BF16)
    os_ref[...] = ys.reshape(os_ref.shape)


def _rmsnorm_slab(x, w, tm=512):
    t, d = x.shape
    tm = min(tm, t)
    cpt = tm // S5_CHUNK
    row = jnp.arange(tm)
    perm = (row[None, :] == ((row % cpt) * S5_CHUNK + row // cpt)[:, None]).astype(BF16)
    return pl.pallas_call(
        _rmsnorm_slab_kernel,
        grid=(t // tm,),
        in_specs=[pl.BlockSpec((tm, d), lambda i: (i, 0)),
                  pl.BlockSpec((1, d), lambda i: (0, 0)),
                  pl.BlockSpec((tm, tm), lambda i: (0, 0))],
        out_specs=[pl.BlockSpec((tm, d), lambda i: (i, 0)),
                   pl.BlockSpec((S5_CHUNK, cpt, d), lambda i: (0, i, 0))],
        out_shape=[jax.ShapeDtypeStruct((t, d), BF16),
                   jax.ShapeDtypeStruct((S5_CHUNK, t // S5_CHUNK, d), BF16)],
        compiler_params=_params(("parallel",)),
    )(x, w.reshape(1, d), perm)


def _inproj_kernel(a_ref, w_ref, o_ref, wb_ref):
    @pl.when(pl.program_id(1) == 0)
    def _():
        wb_ref[...] = w_ref[...].astype(BF16)

    o_ref[...] = jnp.dot(a_ref[...], wb_ref[...], preferred_element_type=F32).astype(o_ref.dtype)


def _inproj(h, w_in, layer, col_start, n_cols, tm=1024, tn=1024, prefetch_w=False):
    t, k = h.shape
    tm, tn = min(tm, t), min(tn, n_cols)
    assert col_start % tn == 0
    c0 = col_start // tn
    w_spec = pl.BlockSpec if prefetch_w else _single
    return pl.pallas_call(
        _inproj_kernel,
        grid=(n_cols // tn, t // tm),
        in_specs=[pl.BlockSpec((tm, k), lambda j, i: (i, 0)),
                  w_spec((None, k, tn), lambda j, i: (layer, 0, c0 + j))],
        out_specs=pl.BlockSpec((tm, tn), lambda j, i: (i, j)),
        out_shape=jax.ShapeDtypeStruct((t, n_cols), BF16),
        scratch_shapes=[pltpu.VMEM((k, tn), BF16)],
        compiler_params=_params(("arbitrary", "arbitrary")),
    )(h, w_in)


def _inproj_t_kernel(h_ref, w_ref, o_ref, wt_ref):
    @pl.when(pl.program_id(1) == 0)
    def _():
        k, tc = w_ref.shape
        for k0 in range(0, k, tc):
            wt_ref[:, k0:k0 + tc] = w_ref[k0:k0 + tc, :].T.astype(BF16)

    acc = lax.dot_general(wt_ref[...], h_ref[...], (((1,), (1,)), ((), ())),
                          preferred_element_type=F32)
    o_ref[...] = acc.astype(BF16).reshape(o_ref.shape)


def _inproj_t(h_slab, w_in, layer, n_ch, tc=1024):
    n_slabs, n_rows, k = h_slab.shape
    tc = min(tc, n_ch)
    return pl.pallas_call(
        _inproj_t_kernel,
        grid=(n_ch // tc, n_slabs),
        in_specs=[pl.BlockSpec((None, n_rows, k), lambda c, j: (j, 0, 0)),
                  _single((None, k, tc), lambda c, j: (layer, 0, c))],
        out_specs=pl.BlockSpec((tc // SSM_GROUP, SSM_GROUP, n_rows), lambda c, j: (c, j, 0)),
        out_shape=jax.ShapeDtypeStruct((n_ch // SSM_GROUP, n_slabs * SSM_GROUP, n_rows), BF16),
        scratch_shapes=[pltpu.VMEM((tc, k), BF16)],
        compiler_params=_params(("arbitrary", "arbitrary")),
    )(h_slab, w_in)


def _outproj_kernel(ys_ref, yr_ref, ws_ref, wr_ref, x_ref, o_ref, wsb_ref, wrb_ref):
    @pl.when(pl.program_id(1) == 0)
    def _():
        wsb_ref[...] = ws_ref[...].astype(BF16)
        wrb_ref[...] = wr_ref[...].astype(BF16)

    acc = jnp.dot(ys_ref[...], wsb_ref[...], preferred_element_type=F32)
    acc = acc + jnp.dot(yr_ref[...], wrb_ref[...], preferred_element_type=F32)
    o_ref[...] = x_ref[...] + acc


def _outproj(y_ssm, y_ret, w_out, layer, x, tm=512, tn=1024):
    t, ds = y_ssm.shape
    _, dr = y_ret.shape
    assert ds == dr
    n = w_out.shape[-1]
    tm, tn = min(tm, t), min(tn, n)
    return pl.pallas_call(
        _outproj_kernel,
        grid=(n // tn, t // tm),
        in_specs=[pl.BlockSpec((tm, ds), lambda j, i: (i, 0)),
                  pl.BlockSpec((tm, dr), lambda j, i: (i, 0)),
                  _single((None, ds, tn), lambda j, i: (layer, 0, j)),
                  _single((None, dr, tn), lambda j, i: (layer, 1, j)),
                  pl.BlockSpec((tm, tn), lambda j, i: (i, j))],
        out_specs=pl.BlockSpec((tm, tn), lambda j, i: (i, j)),
        out_shape=jax.ShapeDtypeStruct((t, n), F32),
        scratch_shapes=[pltpu.VMEM((ds, tn), BF16), pltpu.VMEM((dr, tn), BF16)],
        compiler_params=_params(("arbitrary", "arbitrary")),
    )(y_ssm, y_ret, w_out, w_out, x)


def _s5_matrices(lam_re, lam_im, b_re, b_im, c_re, c_im, d_skip, log_dt, n_chunks):
    g, n = lam_re.shape
    h = SSM_GROUP
    c = S5_CHUNK
    hi = lax.Precision.HIGHEST
    dt = jnp.exp(log_dt)[:, None]
    tau = jnp.arange(c + 1, dtype=F32)[:, None, None]
    mag = jnp.exp(tau * (lam_re * dt)[None])
    ang = tau * (lam_im * dt)[None]
    p_re, p_im = mag * jnp.cos(ang), mag * jnp.sin(ang)
    ab_re, ab_im = p_re[1], p_im[1]
    num_re, num_im = ab_re - 1.0, ab_im
    den = lam_re * lam_re + lam_im * lam_im
    co_re = (num_re * lam_re + num_im * lam_im) / den
    co_im = (num_im * lam_re - num_re * lam_im) / den
    bb_re = co_re[..., None] * b_re - co_im[..., None] * b_im
    bb_im = co_re[..., None] * b_im + co_im[..., None] * b_re
    col = jnp.arange(c * h)
    spread_j = (col[None, :] // h == jnp.arange(c)[:, None]).astype(F32)
    spread_h = (col[None, :] % h == jnp.arange(h)[:, None]).astype(F32)
    q_re = jnp.einsum('jgn,jp->gnp', p_re[:c][::-1], spread_j, precision=hi)
    q_im = jnp.einsum('jgn,jp->gnp', p_im[:c][::-1], spread_j, precision=hi)
    x_re = jnp.einsum('gnk,kp->gnp', bb_re, spread_h, precision=hi)
    x_im = jnp.einsum('gnk,kp->gnp', bb_im, spread_h, precision=hi)
    e_t = jnp.concatenate([q_re * x_re - q_im * x_im, q_re * x_im + q_im * x_re], axis=1)
    c_a = jnp.concatenate([c_re, -c_im], axis=-1)
    c_b = jnp.concatenate([-c_im, -c_re], axis=-1)
    pt_re = p_re[1:].transpose(1, 0, 2)
    pt_im = p_im[1:].transpose(1, 0, 2)
    p_tab = jnp.concatenate([jnp.concatenate([pt_re, pt_re], axis=-1),
                             jnp.concatenate([pt_im, pt_im], axis=-1)], axis=1)
    n_steps = max(1, int(math.log2(n_chunks)))
    assert n_steps <= 8
    ar, ai = p_re[c], p_im[c]
    cols_re, cols_im = [], []
    for _ in range(n_steps):
        cols_re.append(ar)
        cols_im.append(ai)
        ar, ai = ar * ar - ai * ai, 2.0 * ar * ai
    pad = [jnp.zeros((g, n), F32)] * (8 - n_steps)
    stacked = jnp.stack(cols_re + pad + cols_im + pad, axis=0)
    a_pow = jnp.einsum('kgn,km->gnm', stacked, jnp.eye(16, 128, dtype=F32), precision=hi)
    d_col = jnp.tile(d_skip.reshape(g, 1, h), (1, c, 1)).reshape(g, c * h, 1)
    return e_t.astype(BF16), c_a, c_b, p_tab, a_pow, d_col


def _s5_group(u, e, c_a, c_b, p_tab, ap, d_col, n_chunks):
    gsz = c_a.shape[0]
    p = e.shape[1]
    n_pos = p // gsz
    k_rev = jnp.dot(c_a.astype(BF16), e, preferred_element_type=F32)
    lane = lax.broadcasted_iota(jnp.int32, k_rev.shape, 1)
    blocks = []
    for i in range(n_pos):
        shift = (n_pos - 1 - i) * gsz
        if shift == 0:
            blocks.append(k_rev)
        else:
            blocks.append(jnp.where(lane < p - shift, pltpu.roll(k_rev, p - shift, axis=1), 0.0))
    m_t = jnp.concatenate(blocks, axis=0).astype(BF16)
    f_t = jnp.concatenate([c_a * p_tab[i:i + 1] + c_b * p_tab[n_pos + i:n_pos + i + 1]
                           for i in range(n_pos)], axis=0).astype(BF16)

    y = jnp.dot(m_t, u, preferred_element_type=F32)
    s = jnp.dot(e, u, preferred_element_type=F32)
    n = s.shape[0] // 2
    cols = s.shape[1]
    chunk = lax.broadcasted_iota(jnp.int32, (n, cols), 1) % n_chunks
    z_re, z_im = s[:n], s[n:]
    step = 1
    m = 0
    while step < n_chunks:
        keep = chunk >= step
        s_re = jnp.where(keep, pltpu.roll(z_re, step, axis=1), 0.0)
        s_im = jnp.where(keep, pltpu.roll(z_im, step, axis=1), 0.0)
        a_re, a_im = ap[:, m:m + 1], ap[:, 8 + m:9 + m]
        z_re, z_im = z_re + a_re * s_re - a_im * s_im, z_im + a_re * s_im + a_im * s_re
        step *= 2
        m += 1
    first = chunk >= 1
    prev = jnp.concatenate([jnp.where(first, pltpu.roll(z_re, 1, axis=1), 0.0),
                            jnp.where(first, pltpu.roll(z_im, 1, axis=1), 0.0)], axis=0)
    y = y + jnp.dot(f_t, prev.astype(BF16), preferred_element_type=F32)
    return y + u.astype(F32) * d_col


def _s5_kernel(u_ref, e_ref, ca_ref, cb_ref, pt_ref, ap_ref, d_ref, y_ref, *, n_chunks):
    for gi in range(u_ref.shape[0]):
        y = _s5_group(u_ref[gi], e_ref[gi], ca_ref[gi], cb_ref[gi], pt_ref[gi], ap_ref[gi], d_ref[gi],
                      n_chunks)
        y_ref[gi] = y.astype(y_ref.dtype)


def _s5(u_t, mats, layer, n_chunks, gb=16):
    g, p, r = u_t.shape
    e_t, c_a, c_b, p_tab, a_pow, d_col = mats
    n2 = e_t.shape[1]
    gb = min(gb, g)
    steps = g // gb
    spec3 = lambda a, b: pl.BlockSpec((gb, a, b), lambda i: (i, 0, 0))
    tab3 = lambda a, b: pl.BlockSpec((gb, a, b), lambda i: (layer * steps + i, 0, 0))
    return pl.pallas_call(
        functools.partial(_s5_kernel, n_chunks=n_chunks),
        grid=(steps,),
        in_specs=[spec3(p, r), tab3(n2, p), tab3(SSM_GROUP, n2), tab3(SSM_GROUP, n2),
                  tab3(2 * S5_CHUNK, n2), tab3(n2 // 2, 128), tab3(p, 1)],
        out_specs=spec3(p, r),
        out_shape=jax.ShapeDtypeStruct((g, p, r), BF16),
        compiler_params=_params(("parallel",)),
    )(u_t, e_t, c_a, c_b, p_tab, a_pow, d_col)


def _glu_kernel(yt_ref, w_ref, b_ref, nw_ref, g_ref, o_ref, wb_ref):
    @pl.when((pl.program_id(0) == 0) & (pl.program_id(1) == 0))
    def _():
        wb_ref[...] = w_ref[...].astype(BF16)

    groups, gsz, rows = yt_ref.shape
    y = yt_ref[...].reshape(groups * gsz, rows).astype(F32).T
    y = y * (0.5 * (1.0 + jnp.tanh(math.sqrt(2.0 / math.pi) * (y + 0.044715 * (y * y * y)))))
    z = jnp.dot(y.astype(BF16), wb_ref[...], preferred_element_type=F32) + b_ref[...]
    y = y * _sigmoid(z)
    ms = jnp.mean(y * y, axis=-1, keepdims=True)
    y = y * lax.rsqrt(ms + EPS) * nw_ref[...]
    g = g_ref[...].astype(F32)
    o_ref[...] = (y * (g * _sigmoid(g))).astype(o_ref.dtype)


def _glu(y_t, w_glu, layer, b_glu, norm_w, gate_slab, tr=256):
    groups, _, n_rows = y_t.shape
    d = groups * SSM_GROUP
    tr = min(tr, n_rows)
    out = pl.pallas_call(
        _glu_kernel,
        grid=(S5_CHUNK, n_rows // tr),
        in_specs=[pl.BlockSpec((groups, SSM_GROUP, tr), lambda i, r: (0, i, r)),
                  _single((None, d, d), lambda i, r: (layer, 0, 0)),
                  pl.BlockSpec((1, d), lambda i, r: (0, 0)),
                  pl.BlockSpec((1, d), lambda i, r: (0, 0)),
                  pl.BlockSpec((None, tr, d), lambda i, r: (i, r, 0))],
        out_specs=pl.BlockSpec((None, tr, d), lambda i, r: (i, r, 0)),
        out_shape=jax.ShapeDtypeStruct((S5_CHUNK, n_rows, d), BF16),
        scratch_shapes=[pltpu.VMEM((d, d), BF16)],
        compiler_params=_params(("arbitrary", "arbitrary")),
    )(y_t, w_glu, b_glu.reshape(1, d), norm_w.reshape(1, d), gate_slab)
    return _unslab(out)


def _unslab_kernel(p_ref, y_ref, o_ref):
    n_slabs, cpt, d = y_ref.shape
    y = y_ref[...].reshape(n_slabs * cpt, d)
    o_ref[...] = jnp.dot(p_ref[...], y, preferred_element_type=F32).astype(o_ref.dtype)


def _unslab(y_slab, tm=512):
    n_slabs, n_rows, d = y_slab.shape
    t = n_slabs * n_rows
    tm = min(tm, t)
    cpt = tm // n_slabs
    row = jnp.arange(tm)
    perm_t = (row[None, :] == ((row % n_slabs) * cpt + row // n_slabs)[:, None]).astype(BF16)
    return pl.pallas_call(
        _unslab_kernel,
        grid=(t // tm,),
        in_specs=[pl.BlockSpec((tm, tm), lambda i: (0, 0)),
                  pl.BlockSpec((n_slabs, cpt, d), lambda i: (0, i, 0))],
        out_specs=pl.BlockSpec((tm, d), lambda i: (i, 0)),
        out_shape=jax.ShapeDtypeStruct((t, d), y_slab.dtype),
        compiler_params=_params(("parallel",)),
    )(perm_t, y_slab)


def _retention_tables(seq, head_dim):
    pos = jnp.arange(seq, dtype=F32)
    inv_freq = ROPE_BASE ** (-jnp.arange(0, head_dim, 2, dtype=F32) / head_dim)
    ang = pos[:, None] * inv_freq[None, :]
    cos, sin = jnp.cos(ang), jnp.sin(ang)
    log_gamma = jnp.log1p(-jnp.power(2.0, -5.0 - jnp.arange(RET_HEADS, dtype=F32)))
    idx = jnp.arange(RET_CHUNK, dtype=F32)
    rel = idx[:, None] - idx[None, :]
    causal = rel >= 0
    decay_in = jnp.where(causal[None],
                         jnp.exp(jnp.where(causal, rel, 0.0)[None] * log_gamma[:, None, None]),
                         0.0)
    zeta = jnp.exp((RET_CHUNK - 1.0 - idx)[None, :] * log_gamma[:, None])
    xi = jnp.exp((idx + 1.0)[None, :] * log_gamma[:, None])
    gamma_c = jnp.exp(RET_CHUNK * log_gamma)
    bcast = lambda a: jnp.broadcast_to(a[:, :, None], a.shape + (head_dim,))
    gamma_row = jnp.broadcast_to(gamma_c[:, None, None], (RET_HEADS, 1, head_dim))
    return cos, sin, decay_in, bcast(zeta), bcast(xi), gamma_row


def _retention_kernel(q_ref, k_ref, v_ref, g_ref, cos_ref, sin_ref, dec_ref, zeta_ref, xi_ref,
                      gam_ref, nw_ref, o_ref, r_ref, *, n_chunks, head_dim):
    half = head_dim // 2
    scale = head_dim ** -0.5
    r_ref[...] = jnp.zeros_like(r_ref)

    def rope(t, cos, sin):
        t1, t2 = t[:, :half], t[:, half:]
        return jnp.concatenate([t1 * cos - t2 * sin, t2 * cos + t1 * sin], axis=-1)

    def body(c, carry):
        rows = pl.ds(pl.multiple_of(c * RET_CHUNK, RET_CHUNK), RET_CHUNK)
        cos, sin = cos_ref[rows, :], sin_ref[rows, :]
        q = rope(q_ref[rows, :].astype(F32), cos, sin)
        k = rope(k_ref[rows, :].astype(F32), cos, sin) * scale
        v = v_ref[rows, :]
        qb = q.astype(BF16)
        scores = lax.dot_general(qb, k.astype(BF16), (((1,), (1,)), ((), ())),
                                 preferred_element_type=F32) * dec_ref[0]
        out = jnp.dot(scores.astype(BF16), v, preferred_element_type=F32)
        out = out + jnp.dot(qb, r_ref[...].astype(BF16), preferred_element_type=F32) * xi_ref[0]
        kz = (k * zeta_ref[0]).T.astype(BF16)
        r_ref[...] = r_ref[...] * gam_ref[0] + jnp.dot(kz, v, preferred_element_type=F32)
        ms = jnp.mean(out * out, axis=-1, keepdims=True)
        g = g_ref[rows, :].astype(F32)
        out = out * lax.rsqrt(ms + EPS) * nw_ref[...] * (g * _sigmoid(g))
        o_ref[rows, :] = out.astype(o_ref.dtype)
        return carry

    lax.fori_loop(0, n_chunks, body, 0, unroll=True)


def _retention(proj, ret_norm_w, tables, batch, seq, col0):
    t = proj.shape[0]
    d_ret = ret_norm_w.shape[0]
    hd = d_ret // RET_HEADS
    cos, sin, decay_in, zeta, xi, gamma_row = tables
    n_chunks = seq // RET_CHUNK

    def col(group):
        return pl.BlockSpec((seq, hd), lambda b, h: (b, col0 + group * RET_HEADS + h))

    per_head = lambda a, b_: pl.BlockSpec((1, a, b_), lambda b, h: (h, 0, 0))
    const2 = lambda a, b_: pl.BlockSpec((a, b_), lambda b, h: (0, 0))
    return pl.pallas_call(
        functools.partial(_retention_kernel, n_chunks=n_chunks, head_dim=hd),
        grid=(batch, RET_HEADS),
        in_specs=[col(0), col(1), col(2), col(3),
                  const2(seq, hd // 2), const2(seq, hd // 2),
                  per_head(RET_CHUNK, RET_CHUNK), per_head(RET_CHUNK, hd), per_head(RET_CHUNK, hd),
                  per_head(1, hd),
                  pl.BlockSpec((1, hd), lambda b, h: (0, h))],
        out_specs=pl.BlockSpec((seq, hd), lambda b, h: (b, h)),
        out_shape=jax.ShapeDtypeStruct((t, d_ret), BF16),
        scratch_shapes=[pltpu.VMEM((hd, hd), F32)],
        compiler_params=_params(("parallel", "parallel")),
    )(proj, proj, proj, proj, cos, sin, decay_in, zeta, xi, gamma_row, ret_norm_w.reshape(1, d_ret))


def _layer(x2, batch, seq, layer, norm_w, w_in, s5_mats, w_glu, b_glu, ssm_norm_w, ret_norm_w, w_out,
           ret_tables):
    t, d_model = x2.shape
    d_ssm = ssm_norm_w.shape[0]
    n_chunks = seq // S5_CHUNK
    n_rows = batch * n_chunks
    n_proj = w_in.shape[-1]

    h, h_slab = _rmsnorm_slab(x2, norm_w)
    proj = _inproj(h, w_in, layer, 2 * d_ssm, n_proj - 2 * d_ssm, tm=512, prefetch_w=True)
    gate_slab = _inproj(h_slab.reshape(t, d_model), w_in, layer, d_ssm, d_ssm, tm=512, prefetch_w=True)
    gate_slab = gate_slab.reshape(S5_CHUNK, n_rows, d_ssm)
    u_t = _inproj_t(h_slab, w_in, layer, d_ssm)
    y_t = _s5(u_t, s5_mats, layer, n_chunks)
    y_ssm = _glu(y_t, w_glu, layer, b_glu, ssm_norm_w, gate_slab)
    y_ret = _retention(proj, ret_norm_w, ret_tables, batch, seq, 0)
    return _outproj(y_ssm, y_ret, w_out, layer, x2)


def kernel(x, norm_w, w_in, ssm_lambda_re, ssm_lambda_im, ssm_b_re, ssm_b_im, ssm_c_re, ssm_c_im,
           ssm_d, ssm_log_dt, ssm_w_glu, ssm_b_glu, ssm_norm_w, ret_norm_w, w_out, final_norm_w):
    batch, seq, d_model = x.shape
    depth = norm_w.shape[0]
    d_ret = ret_norm_w.shape[1]
    ret_tables = _retention_tables(seq, d_ret // RET_HEADS)
    flat = lambda a: a.reshape((-1,) + a.shape[2:])
    s5_mats = _s5_matrices(flat(ssm_lambda_re), flat(ssm_lambda_im), flat(ssm_b_re), flat(ssm_b_im),
                           flat(ssm_c_re), flat(ssm_c_im), ssm_d.reshape(-1), ssm_log_dt.reshape(-1),
                           seq // S5_CHUNK)
    h = x.reshape(batch * seq, d_model)
    for layer in range(depth):
        h = _layer(h, batch, seq, layer, norm_w[layer], w_in, s5_mats,
                   ssm_w_glu, ssm_b_glu[layer], ssm_norm_w[layer], ret_norm_w[layer],
                   w_out, ret_tables)
    out = _rmsnorm(h, final_norm_w, F32)
    return out.reshape(batch, seq, d_model)
```

```python
import functools
import math

import jax
import jax.numpy as jnp
from jax import lax
from jax.experimental import pallas as pl
from jax.experimental.pallas import tpu as pltpu

F32 = jnp.float32
BF16 = jnp.bfloat16

EPS = 1e-6
SSM_GROUP = 16
RET_HEADS = 8
RET_CHUNK = 128
ROPE_BASE = 10000.0
S5_CHUNK = 32

VMEM_LIMIT = 56 * 1024 * 1024


def _params(sem):
    return pltpu.CompilerParams(dimension_semantics=sem, vmem_limit_bytes=VMEM_LIMIT)


def _sigmoid(x):
    return 0.5 * (1.0 + jnp.tanh(0.5 * x))


def _single(block_shape, index_map):
    return pl.BlockSpec(block_shape, index_map, pipeline_mode=pl.Buffered(1))


def _rmsnorm_kernel(x_ref, w_ref, o_ref):
    x = x_ref[...]
    ms = jnp.mean(x * x, axis=-1, keepdims=True)
    o_ref[...] = (x * lax.rsqrt(ms + EPS) * w_ref[...]).astype(o_ref.dtype)


def _rmsnorm(x, w, out_dtype, tm=512):
    t, d = x.shape
    tm = min(tm, t)
    return pl.pallas_call(
        _rmsnorm_kernel,
        grid=(t // tm,),
        in_specs=[pl.BlockSpec((tm, d), lambda i: (i, 0)),
                  pl.BlockSpec((1, d), lambda i: (0, 0))],
        out_specs=pl.BlockSpec((tm, d), lambda i: (i, 0)),
        out_shape=jax.ShapeDtypeStruct((t, d), out_dtype),
        compiler_params=_params(("parallel",)),
    )(x, w.reshape(1, d))


def _rmsnorm_slab_kernel(x_ref, w_ref, p_ref, o_ref, os_ref):
    x = x_ref[...]
    ms = jnp.mean(x * x, axis=-1, keepdims=True)
    y = (x * lax.rsqrt(ms + EPS) * w_ref[...]).astype(BF16)
    o_ref[...] = y
    ys = jnp.dot(p_ref[...], y, preferred_element_type=F32).astype(BF16)
    os_ref[...] = ys.reshape(os_ref.shape)


def _rmsnorm_slab(x, w, tm=512):
    t, d = x.shape
    tm = min(tm, t)
    cpt = tm // S5_CHUNK
    row = jnp.arange(tm)
    perm = (row[None, :] == ((row % cpt) * S5_CHUNK + row // cpt)[:, None]).astype(BF16)
    return pl.pallas_call(
        _rmsnorm_slab_kernel,
        grid=(t // tm,),
        in_specs=[pl.BlockSpec((tm, d), lambda i: (i, 0)),
                  pl.BlockSpec((1, d), lambda i: (0, 0)),
                  pl.BlockSpec((tm, tm), lambda i: (0, 0))],
        out_specs=[pl.BlockSpec((tm, d), lambda i: (i, 0)),
                   pl.BlockSpec((S5_CHUNK, cpt, d), lambda i: (0, i, 0))],
        out_shape=[jax.ShapeDtypeStruct((t, d), BF16),
                   jax.ShapeDtypeStruct((S5_CHUNK, t // S5_CHUNK, d), BF16)],
        compiler_params=_params(("parallel",)),
    )(x, w.reshape(1, d), perm)


def _inproj_kernel(a_ref, w_ref, o_ref, wb_ref):
    @pl.when(pl.program_id(1) == 0)
    def _():
        wb_ref[...] = w_ref[...].astype(BF16)

    o_ref[...] = jnp.dot(a_ref[...], wb_ref[...], preferred_element_type=F32).astype(o_ref.dtype)


def _inproj(h, w_in, layer, col_start, n_cols, tm=1024, tn=1024, prefetch_w=False):
    t, k = h.shape
    tm, tn = min(tm, t), min(tn, n_cols)
    assert col_start % tn == 0
    c0 = col_start // tn
    w_spec = pl.BlockSpec if prefetch_w else _single
    return pl.pallas_call(
        _inproj_kernel,
        grid=(n_cols // tn, t // tm),
        in_specs=[pl.BlockSpec((tm, k), lambda j, i: (i, 0)),
                  w_spec((None, k, tn), lambda j, i: (layer, 0, c0 + j))],
        out_specs=pl.BlockSpec((tm, tn), lambda j, i: (i, j)),
        out_shape=jax.ShapeDtypeStruct((t, n_cols), BF16),
        scratch_shapes=[pltpu.VMEM((k, tn), BF16)],
        compiler_params=_params(("arbitrary", "arbitrary")),
    )(h, w_in)


def _inproj_t_kernel(h_ref, w_ref, o_ref, wt_ref):
    @pl.when(pl.program_id(1) == 0)
    def _():
        k, tc = w_ref.shape
        for k0 in range(0, k, tc):
            wt_ref[:, k0:k0 + tc] = w_ref[k0:k0 + tc, :].T.astype(BF16)

    acc = lax.dot_general(wt_ref[...], h_ref[...], (((1,), (1,)), ((), ())),
                          preferred_element_type=F32)
    o_ref[...] = acc.astype(BF16).reshape(o_ref.shape)


def _inproj_t(h_slab, w_in, layer, n_ch, tc=1024):
    n_slabs, n_rows, k = h_slab.shape
    tc = min(tc, n_ch)
    return pl.pallas_call(
        _inproj_t_kernel,
        grid=(n_ch // tc, n_slabs),
        in_specs=[pl.BlockSpec((None, n_rows, k), lambda c, j: (j, 0, 0)),
                  _single((None, k, tc), lambda c, j: (layer, 0, c))],
        out_specs=pl.BlockSpec((tc // SSM_GROUP, SSM_GROUP, n_rows), lambda c, j: (c, j, 0)),
        out_shape=jax.ShapeDtypeStruct((n_ch // SSM_GROUP, n_slabs * SSM_GROUP, n_rows), BF16),
        scratch_shapes=[pltpu.VMEM((tc, k), BF16)],
        compiler_params=_params(("arbitrary", "arbitrary")),
    )(h_slab, w_in)


def _outproj_kernel(ys_ref, yr_ref, ws_ref, wr_ref, x_ref, o_ref, wsb_ref, wrb_ref):
    @pl.when(pl.program_id(1) == 0)
    def _():
        wsb_ref[...] = ws_ref[...].astype(BF16)
        wrb_ref[...] = wr_ref[...].astype(BF16)

    acc = jnp.dot(ys_ref[...], wsb_ref[...], preferred_element_type=F32)
    acc = acc + jnp.dot(yr_ref[...], wrb_ref[...], preferred_element_type=F32)
    o_ref[...] = x_ref[...] + acc


def _outproj(y_ssm, y_ret, w_out, layer, x, tm=512, tn=1024):
    t, ds = y_ssm.shape
    _, dr = y_ret.shape
    assert ds == dr
    n = w_out.shape[-1]
    tm, tn = min(tm, t), min(tn, n)
    return pl.pallas_call(
        _outproj_kernel,
        grid=(n // tn, t // tm),
        in_specs=[pl.BlockSpec((tm, ds), lambda j, i: (i, 0)),
                  pl.BlockSpec((tm, dr), lambda j, i: (i, 0)),
                  _single((None, ds, tn), lambda j, i: (layer, 0, j)),
                  _single((None, dr, tn), lambda j, i: (layer, 1, j)),
                  pl.BlockSpec((tm, tn), lambda j, i: (i, j))],
        out_specs=pl.BlockSpec((tm, tn), lambda j, i: (i, j)),
        out_shape=jax.ShapeDtypeStruct((t, n), F32),
        scratch_shapes=[pltpu.VMEM((ds, tn), BF16), pltpu.VMEM((dr, tn), BF16)],
        compiler_params=_params(("arbitrary", "arbitrary")),
    )(y_ssm, y_ret, w_out, w_out, x)


def _s5_matrices(lam_re, lam_im, b_re, b_im, c_re, c_im, d_skip, log_dt, n_chunks):
    g, n = lam_re.shape
    h = SSM_GROUP
    c = S5_CHUNK
    hi = lax.Precision.HIGHEST
    dt = jnp.exp(log_dt)[:, None]
    tau = jnp.arange(c + 1, dtype=F32)[:, None, None]
    mag = jnp.exp(tau * (lam_re * dt)[None])
    ang = tau * (lam_im * dt)[None]
    p_re, p_im = mag * jnp.cos(ang), mag * jnp.sin(ang)
    ab_re, ab_im = p_re[1], p_im[1]
    num_re, num_im = ab_re - 1.0, ab_im
    den = lam_re * lam_re + lam_im * lam_im
    co_re = (num_re * lam_re + num_im * lam_im) / den
    co_im = (num_im * lam_re - num_re * lam_im) / den
    bb_re = co_re[..., None] * b_re - co_im[..., None] * b_im
    bb_im = co_re[..., None] * b_im + co_im[..., None] * b_re
    col = jnp.arange(c * h)
    spread_j = (col[None, :] // h == jnp.arange(c)[:, None]).astype(F32)
    spread_h = (col[None, :] % h == jnp.arange(h)[:, None]).astype(F32)
    q_re = jnp.einsum('jgn,jp->gnp', p_re[:c][::-1], spread_j, precision=hi)
    q_im = jnp.einsum('jgn,jp->gnp', p_im[:c][::-1], spread_j, precision=hi)
    x_re = jnp.einsum('gnk,kp->gnp', bb_re, spread_h, precision=hi)
    x_im = jnp.einsum('gnk,kp->gnp', bb_im, spread_h, precision=hi)
    e_t = jnp.concatenate([q_re * x_re - q_im * x_im, q_re * x_im + q_im * x_re], axis=1)
    c_a = jnp.concatenate([c_re, -c_im], axis=-1)
    c_b = jnp.concatenate([-c_im, -c_re], axis=-1)
    pt_re = p_re[1:].transpose(1, 0, 2)
    pt_im = p_im[1:].transpose(1, 0, 2)
    p_tab = jnp.concatenate([jnp.concatenate([pt_re, pt_re], axis=-1),
                             jnp.concatenate([pt_im, pt_im], axis=-1)], axis=1)
    n_steps = max(1, int(math.log2(n_chunks)))
    assert n_steps <= 8
    ar, ai = p_re[c], p_im[c]
    cols_re, cols_im = [], []
    for _ in range(n_steps):
        cols_re.append(ar)
        cols_im.append(ai)
        ar, ai = ar * ar - ai * ai, 2.0 * ar * ai
    pad = [jnp.zeros((g, n), F32)] * (8 - n_steps)
    stacked = jnp.stack(cols_re + pad + cols_im + pad, axis=0)
    a_pow = jnp.einsum('kgn,km->gnm', stacked, jnp.eye(16, 128, dtype=F32), precision=hi)
    d_col = jnp.tile(d_skip.reshape(g, 1, h), (1, c, 1)).reshape(g, c * h, 1)
    return e_t.astype(BF16), c_a, c_b, p_tab, a_pow, d_col


def _s5_group(u, e, c_a, c_b, p_tab, ap, d_col, n_chunks):
    gsz = c_a.shape[0]
    p = e.shape[1]
    n_pos = p // gsz
    k_rev = jnp.dot(c_a.astype(BF16), e, preferred_element_type=F32)
    lane = lax.broadcasted_iota(jnp.int32, k_rev.shape, 1)
    blocks = []
    for i in range(n_pos):
        shift = (n_pos - 1 - i) * gsz
        if shift == 0:
            blocks.append(k_rev)
        else:
            blocks.append(jnp.where(lane < p - shift, pltpu.roll(k_rev, p - shift, axis=1), 0.0))
    m_t = jnp.concatenate(blocks, axis=0).astype(BF16)
    f_t = jnp.concatenate([c_a * p_tab[i:i + 1] + c_b * p_tab[n_pos + i:n_pos + i + 1]
                           for i in range(n_pos)], axis=0).astype(BF16)

    y = jnp.dot(m_t, u, preferred_element_type=F32)
    s = jnp.dot(e, u, preferred_element_type=F32)
    n = s.shape[0] // 2
    cols = s.shape[1]
    chunk = lax.broadcasted_iota(jnp.int32, (n, cols), 1) % n_chunks
    z_re, z_im = s[:n], s[n:]
    step = 1
    m = 0
    while step < n_chunks:
        keep = chunk >= step
        s_re = jnp.where(keep, pltpu.roll(z_re, step, axis=1), 0.0)
        s_im = jnp.where(keep, pltpu.roll(z_im, step, axis=1), 0.0)
        a_re, a_im = ap[:, m:m + 1], ap[:, 8 + m:9 + m]
        z_re, z_im = z_re + a_re * s_re - a_im * s_im, z_im + a_re * s_im + a_im * s_re
        step *= 2
        m += 1
    first = chunk >= 1
    prev = jnp.concatenate([jnp.where(first, pltpu.roll(z_re, 1, axis=1), 0.0),
                            jnp.where(first, pltpu.roll(z_im, 1, axis=1), 0.0)], axis=0)
    y = y + jnp.dot(f_t, prev.astype(BF16), preferred_element_type=F32)
    return y + u.astype(F32) * d_col


def _s5_kernel(u_ref, e_ref, ca_ref, cb_ref, pt_ref, ap_ref, d_ref, y_ref, *, n_chunks):
    for gi in range(u_ref.shape[0]):
        y = _s5_group(u_ref[gi], e_ref[gi], ca_ref[gi], cb_ref[gi], pt_ref[gi], ap_ref[gi], d_ref[gi],
                      n_chunks)
        y_ref[gi] = y.astype(y_ref.dtype)


def _s5(u_t, mats, layer, n_chunks, gb=16):
    g, p, r = u_t.shape
    e_t, c_a, c_b, p_tab, a_pow, d_col = mats
    n2 = e_t.shape[1]
    gb = min(gb, g)
    steps = g // gb
    spec3 = lambda a, b: pl.BlockSpec((gb, a, b), lambda i: (i, 0, 0))
    tab3 = lambda a, b: pl.BlockSpec((gb, a, b), lambda i: (layer * steps + i, 0, 0))
    return pl.pallas_call(
        functools.partial(_s5_kernel, n_chunks=n_chunks),
        grid=(steps,),
        in_specs=[spec3(p, r), tab3(n2, p), tab3(SSM_GROUP, n2), tab3(SSM_GROUP, n2),
                  tab3(2 * S5_CHUNK, n2), tab3(n2 // 2, 128), tab3(p, 1)],
        out_specs=spec3(p, r),
        out_shape=jax.ShapeDtypeStruct((g, p, r), BF16),
        compiler_params=_params(("parallel",)),
    )(u_t, e_t, c_a, c_b, p_tab, a_pow, d_col)


def _glu_kernel(yt_ref, w_ref, b_ref, nw_ref, g_ref, o_ref, wb_ref):
    @pl.when((pl.program_id(0) == 0) & (pl.program_id(1) == 0))
    def _():
        wb_ref[...] = w_ref[...].astype(BF16)

    groups, gsz, rows = yt_ref.shape
    y = yt_ref[...].reshape(groups * gsz, rows).astype(F32).T
    y = y * (0.5 * (1.0 + jnp.tanh(math.sqrt(2.0 / math.pi) * (y + 0.044715 * (y * y * y)))))
    z = jnp.dot(y.astype(BF16), wb_ref[...], preferred_element_type=F32) + b_ref[...]
    y = y * _sigmoid(z)
    ms = jnp.mean(y * y, axis=-1, keepdims=True)
    y = y * lax.rsqrt(ms + EPS) * nw_ref[...]
    g = g_ref[...].astype(F32)
    o_ref[...] = (y * (g * _sigmoid(g))).astype(o_ref.dtype)


def _glu(y_t, w_glu, layer, b_glu, norm_w, gate_slab, tr=256):
    groups, _, n_rows = y_t.shape
    d = groups * SSM_GROUP
    tr = min(tr, n_rows)
    out = pl.pallas_call(
        _glu_kernel,
        grid=(S5_CHUNK, n_rows // tr),
        in_specs=[pl.BlockSpec((groups, SSM_GROUP, tr), lambda i, r: (0, i, r)),
                  _single((None, d, d), lambda i, r: (layer, 0, 0)),
                  pl.BlockSpec((1, d), lambda i, r: (0, 0)),
                  pl.BlockSpec((1, d), lambda i, r: (0, 0)),
                  pl.BlockSpec((None, tr, d), lambda i, r: (i, r, 0))],
        out_specs=pl.BlockSpec((None, tr, d), lambda i, r: (i, r, 0)),
        out_shape=jax.ShapeDtypeStruct((S5_CHUNK, n_rows, d), BF16),
        scratch_shapes=[pltpu.VMEM((d, d), BF16)],
        compiler_params=_params(("arbitrary", "arbitrary")),
    )(y_t, w_glu, b_glu.reshape(1, d), norm_w.reshape(1, d), gate_slab)
    return _unslab(out)


def _unslab_kernel(p_ref, y_ref, o_ref):
    n_slabs, cpt, d = y_ref.shape
    y = y_ref[...].reshape(n_slabs * cpt, d)
    o_ref[...] = jnp.dot(p_ref[...], y, preferred_element_type=F32).astype(o_ref.dtype)


def _unslab(y_slab, tm=512):
    n_slabs, n_rows, d = y_slab.shape
    t = n_slabs * n_rows
    tm = min(tm, t)
    cpt = tm // n_slabs
    row = jnp.arange(tm)
    perm_t = (row[None, :] == ((row % n_slabs) * cpt + row // n_slabs)[:, None]).astype(BF16)
    return pl.pallas_call(
        _unslab_kernel,
        grid=(t // tm,),
        in_specs=[pl.BlockSpec((tm, tm), lambda i: (0, 0)),
                  pl.BlockSpec((n_slabs, cpt, d), lambda i: (0, i, 0))],
        out_specs=pl.BlockSpec((tm, d), lambda i: (i, 0)),
        out_shape=jax.ShapeDtypeStruct((t, d), y_slab.dtype),
        compiler_params=_params(("parallel",)),
    )(perm_t, y_slab)


def _retention_tables(seq, head_dim):
    pos = jnp.arange(seq, dtype=F32)
    inv_freq = ROPE_BASE ** (-jnp.arange(0, head_dim, 2, dtype=F32) / head_dim)
    ang = pos[:, None] * inv_freq[None, :]
    cos, sin = jnp.cos(ang), jnp.sin(ang)
    log_gamma = jnp.log1p(-jnp.power(2.0, -5.0 - jnp.arange(RET_HEADS, dtype=F32)))
    idx = jnp.arange(RET_CHUNK, dtype=F32)
    rel = idx[:, None] - idx[None, :]
    causal = rel >= 0
    decay_in = jnp.where(causal[None],
                         jnp.exp(jnp.where(causal, rel, 0.0)[None] * log_gamma[:, None, None]),
                         0.0)
    zeta = jnp.exp((RET_CHUNK - 1.0 - idx)[None, :] * log_gamma[:, None])
    xi = jnp.exp((idx + 1.0)[None, :] * log_gamma[:, None])
    gamma_c = jnp.exp(RET_CHUNK * log_gamma)
    bcast = lambda a: jnp.broadcast_to(a[:, :, None], a.shape + (head_dim,))
    gamma_row = jnp.broadcast_to(gamma_c[:, None, None], (RET_HEADS, 1, head_dim))
    return cos, sin, decay_in, bcast(zeta), bcast(xi), gamma_row


def _retention_kernel(q_ref, k_ref, v_ref, g_ref, cos_ref, sin_ref, dec_ref, zeta_ref, xi_ref,
                      gam_ref, nw_ref, o_ref, r_ref, *, n_chunks, head_dim):
    half = head_dim // 2
    scale = head_dim ** -0.5
    r_ref[...] = jnp.zeros_like(r_ref)

    def rope(t, cos, sin):
        t1, t2 = t[:, :half], t[:, half:]
        return jnp.concatenate([t1 * cos - t2 * sin, t2 * cos + t1 * sin], axis=-1)

    def body(c, carry):
        rows = pl.ds(pl.multiple_of(c * RET_CHUNK, RET_CHUNK), RET_CHUNK)
        cos, sin = cos_ref[rows, :], sin_ref[rows, :]
        q = rope(q_ref[rows, :].astype(F32), cos, sin)
        k = rope(k_ref[rows, :].astype(F32), cos, sin) * scale
        v = v_ref[rows, :]
        qb = q.astype(BF16)
        scores = lax.dot_general(qb, k.astype(BF16), (((1,), (1,)), ((), ())),
                                 preferred_element_type=F32) * dec_ref[0]
        out = jnp.dot(scores.astype(BF16), v, preferred_element_type=F32)
        out = out + jnp.dot(qb, r_ref[...].astype(BF16), preferred_element_type=F32) * xi_ref[0]
        kz = (k * zeta_ref[0]).T.astype(BF16)
        r_ref[...] = r_ref[...] * gam_ref[0] + jnp.dot(kz, v, preferred_element_type=F32)
        ms = jnp.mean(out * out, axis=-1, keepdims=True)
        g = g_ref[rows, :].astype(F32)
        out = out * lax.rsqrt(ms + EPS) * nw_ref[...] * (g * _sigmoid(g))
        o_ref[rows, :] = out.astype(o_ref.dtype)
        return carry

    lax.fori_loop(0, n_chunks, body, 0, unroll=True)


def _retention(proj, ret_norm_w, tables, batch, seq, col0):
    t = proj.shape[0]
    d_ret = ret_norm_w.shape[0]
    hd = d_ret // RET_HEADS
    cos, sin, decay_in, zeta, xi, gamma_row = tables
    n_chunks = seq // RET_CHUNK

    def col(group):
        return pl.BlockSpec((seq, hd), lambda b, h: (b, col0 + group * RET_HEADS + h))

    per_head = lambda a, b_: pl.BlockSpec((1, a, b_), lambda b, h: (h, 0, 0))
    const2 = lambda a, b_: pl.BlockSpec((a, b_), lambda b, h: (0, 0))
    return pl.pallas_call(
        functools.partial(_retention_kernel, n_chunks=n_chunks, head_dim=hd),
        grid=(batch, RET_HEADS),
        in_specs=[col(0), col(1), col(2), col(3),
                  const2(seq, hd // 2), const2(seq, hd // 2),
                  per_head(RET_CHUNK, RET_CHUNK), per_head(RET_CHUNK, hd), per_head(RET_CHUNK, hd),
                  per_head(1, hd),
                  pl.BlockSpec((1, hd), lambda b, h: (0, h))],
        out_specs=pl.BlockSpec((seq, hd), lambda b, h: (b, h)),
        out_shape=jax.ShapeDtypeStruct((t, d_ret), BF16),
        scratch_shapes=[pltpu.VMEM((hd, hd), F32)],
        compiler_params=_params(("parallel", "parallel")),
    )(proj, proj, proj, proj, cos, sin, decay_in, zeta, xi, gamma_row, ret_norm_w.reshape(1, d_ret))


def _layer(x2, batch, seq, layer, norm_w, w_in, s5_mats, w_glu, b_glu, ssm_norm_w, ret_norm_w, w_out,
           ret_tables):
    t, d_model = x2.shape
    d_ssm = ssm_norm_w.shape[0]
    n_chunks = seq // S5_CHUNK
    n_rows = batch * n_chunks
    n_proj = w_in.shape[-1]

    h, h_slab = _rmsnorm_slab(x2, norm_w)
    proj = _inproj(h, w_in, layer, 2 * d_ssm, n_proj - 2 * d_ssm, tm=512, prefetch_w=True)
    gate_slab = _inproj(h_slab.reshape(t, d_model), w_in, layer, d_ssm, d_ssm, tm=512, prefetch_w=True)
    gate_slab = gate_slab.reshape(S5_CHUNK, n_rows, d_ssm)
    u_t = _inproj_t(h_slab, w_in, layer, d_ssm)
    y_t = _s5(u_t, s5_mats, layer, n_chunks)
    y_ssm = _glu(y_t, w_glu, layer, b_glu, ssm_norm_w, gate_slab)
    y_ret = _retention(proj, ret_norm_w, ret_tables, batch, seq, 0)
    return _outproj(y_ssm, y_ret, w_out, layer, x2)


def kernel(x, norm_w, w_in, ssm_lambda_re, ssm_lambda_im, ssm_b_re, ssm_b_im, ssm_c_re, ssm_c_im,
           ssm_d, ssm_log_dt, ssm_w_glu, ssm_b_glu, ssm_norm_w, ret_norm_w, w_out, final_norm_w):
    batch, seq, d_model = x.shape
    depth = norm_w.shape[0]
    d_ret = ret_norm_w.shape[1]
    ret_tables = _retention_tables(seq, d_ret // RET_HEADS)
    flat = lambda a: a.reshape((-1,) + a.shape[2:])
    s5_mats = _s5_matrices(flat(ssm_lambda_re), flat(ssm_lambda_im), flat(ssm_b_re), flat(ssm_b_im),
                           flat(ssm_c_re), flat(ssm_c_im), ssm_d.reshape(-1), ssm_log_dt.reshape(-1),
                           seq // S5_CHUNK)
    h = x.reshape(batch * seq, d_model)
    for layer in range(depth):
        h = _layer(h, batch, seq, layer, norm_w[layer], w_in, s5_mats,
                   ssm_w_glu, ssm_b_glu[layer], ssm_norm_w[layer], ret_norm_w[layer],
                   w_out, ret_tables)
    out = _rmsnorm(h, final_norm_w, F32)
    return out.reshape(batch, seq, d_model)
```
